```python
import jax, jax.numpy as jnp
from jax import lax
import numpy as np

D_MODEL = 2048
BATCH = 16
SEQ = 2048
DEPTH = 1
DEC_BATCH = 32
DEC_SEQ = 4
PAST_LEN = 16384
PAGE_SIZE = 128

MIX_WIDTH = D_MODEL
HEAD_DIM = 128
A_WIDTH = MIX_WIDTH // 2
A_HEADS = A_WIDTH // HEAD_DIM
DILATED_CONFIGS = ((128, 1), (512, 4), (2048, 16))
A_BUF = max(w for w, _ in DILATED_CONFIGS)
B_WIDTH = MIX_WIDTH - A_WIDTH
B_HEADS = 4
B_DV = B_WIDTH // B_HEADS
B_DK = B_DV // 2
GATE_RANK = 16
GATE_TAU = 16.0
GLA_CHUNK = 64
D_FF = 4 * D_MODEL
RMS_EPS = 1e-6
N_IN = 3 * A_WIDTH + 2 * B_HEADS * B_DK + 2 * B_WIDTH + GATE_RANK

kernel_name = "hymba_dilated_gla_decoder_step"


def rms_f32(x, g):
    xf = x.astype(jnp.float32)
    return xf * lax.rsqrt(jnp.mean(xf * xf, axis=-1, keepdims=True) + RMS_EPS) * g.astype(jnp.float32)


def rmsnorm(x, g):
    return rms_f32(x, g).astype(x.dtype)


def mixer_inputs(h, w_in, q_g, k_g, w_gate2, b_gate):
    bn, t, _ = h.shape
    proj = h @ w_in
    splits = [int(s) for s in np.cumsum([A_WIDTH, A_WIDTH, A_WIDTH, B_HEADS * B_DK,
                                         B_HEADS * B_DK, B_WIDTH, B_WIDTH])]
    qa, ka, va, qb, kb, vb, rb, ga = jnp.split(proj, splits, axis=-1)
    qa = rms_f32(qa.reshape(bn, t, A_HEADS, HEAD_DIM), q_g)
    ka = rms_f32(ka.reshape(bn, t, A_HEADS, HEAD_DIM), k_g)
    va = va.reshape(bn, t, A_HEADS, HEAD_DIM).astype(jnp.float32)
    qb = qb.reshape(bn, t, B_HEADS, B_DK).astype(jnp.float32) * (B_DK ** -0.5)
    kb = kb.reshape(bn, t, B_HEADS, B_DK).astype(jnp.float32)
    vb = vb.reshape(bn, t, B_HEADS, B_DV).astype(jnp.float32)
    z = (ga @ w_gate2 + b_gate).astype(jnp.float32)
    log_a = (jax.nn.log_sigmoid(z) / GATE_TAU).reshape(bn, t, B_HEADS, B_DK)
    return qa, ka, va, qb, kb, vb, rb, log_a


def dilated_window_full(q, k, v, window, dilation):
    bn, s_len, h, dh = q.shape
    blk = window // dilation
    span = blk * dilation
    sp = -(-s_len // span) * span
    pad = sp - s_len
    nb = sp // span

    def to_blocks(t):
        t = jnp.pad(t, ((0, 0), (0, pad), (0, 0), (0, 0)))
        return t.reshape(bn, nb, blk, dilation, h, dh).transpose(0, 3, 1, 2, 4, 5)

    qb, kb, vb = to_blocks(q), to_blocks(k), to_blocks(v)

    def with_prev(t):
        prev = jnp.pad(t, ((0, 0), (0, 0), (1, 0), (0, 0), (0, 0), (0, 0)))[:, :, :-1]
        return jnp.concatenate([prev, t], axis=3)

    kk, vv = with_prev(kb), with_prev(vb)
    s = jnp.einsum('brnqhd,brnkhd->brnhqk', qb, kk) * (HEAD_DIM ** -0.5)
    qi = jnp.arange(blk)[:, None]
    kj = jnp.arange(2 * blk)[None, :]
    dist = qi + blk - kj
    band = (dist >= 0) & (dist <= blk)
    valid = band[None] & ((jnp.arange(nb)[:, None, None] > 0) | (kj >= blk)[None])
    s = jnp.where(valid[None, None, :, None], s, -jnp.inf)
    lse = jax.nn.logsumexp(s, axis=-1)
    p = jnp.exp(s - lse[..., None])
    o = jnp.einsum('brnhqk,brnkhd->brnqhd', p, vv)
    o = o.transpose(0, 2, 3, 1, 4, 5).reshape(bn, sp, h, dh)[:, :s_len]
    lse = lse.transpose(0, 2, 4, 1, 3).reshape(bn, sp, h)[:, :s_len]
    return o, lse


def dilated_window_step(q, k_ext, v_ext, window, dilation):
    t = q.shape[1]
    l = k_ext.shape[1] - t
    n_keys = window // dilation + 1
    idx = l + jnp.arange(t)[:, None] - dilation * jnp.arange(n_keys)[None, :]
    valid = idx >= 0
    idxc = jnp.maximum(idx, 0)
    kg = k_ext[:, idxc]
    vg = v_ext[:, idxc]
    s = jnp.einsum('bthd,btjhd->bthj', q, kg) * (HEAD_DIM ** -0.5)
    s = jnp.where(valid[None, :, None, :], s, -jnp.inf)
    lse = jax.nn.logsumexp(s, axis=-1)
    p = jnp.exp(s - lse[..., None])
    o = jnp.einsum('bthj,btjhd->bthd', p, vg)
    return o, lse


def combine_by_denominator(outs, lses):
    o = jnp.stack(outs)
    w = jax.nn.softmax(jnp.stack(lses), axis=0)
    return jnp.sum(o * w[..., None], axis=0)


def gla_chunked(q, k, v, log_a, s0):
    bn, t, h, _ = q.shape
    c = GLA_CHUNK
    tp = -(-t // c) * c
    pad = tp - t
    n = tp // c

    def chunks(x):
        x = jnp.pad(x, ((0, 0), (0, pad), (0, 0), (0, 0)))
        return x.reshape(bn, n, c, h, x.shape[-1])

    q, k, v, log_a = chunks(q), chunks(k), chunks(v), chunks(log_a)
    b = jnp.cumsum(log_a, axis=2)
    b_last = b[:, :, -1]
    q_dec = q * jnp.exp(b)
    k_inv = k * jnp.exp(-b)
    causal = jnp.tril(jnp.ones((c, c), dtype=bool))
    att = jnp.einsum('bnihk,bnjhk->bnhij', q_dec, k_inv)
    att = jnp.where(causal, att, 0.0)
    o = jnp.einsum('bnhij,bnjhv->bnihv', att, v)
    k_end = k * jnp.exp(b_last[:, :, None] - b)
    delta = jnp.einsum('bnjhk,bnjhv->bnhkv', k_end, v)
    decay = jnp.exp(b_last)

    def step(s, xs):
        dec, dlt = xs
        return dec[..., None] * s + dlt, s

    s_fin, s_prev = lax.scan(step, s0, (jnp.moveaxis(decay, 1, 0), jnp.moveaxis(delta, 1, 0)))
    o = o + jnp.einsum('bnihk,nbhkv->bnihv', q_dec, s_prev)
    return o.reshape(bn, tp, h, v.shape[-1])[:, :t], s_fin


def layer_output(x, oa, ob, rb, gla_norm_g, w_out, mlp_norm_g, w_up, w_down):
    bn, t, _ = x.shape
    ob = rms_f32(ob, gla_norm_g).reshape(bn, t, B_WIDTH) * jax.nn.silu(rb.astype(jnp.float32))
    mix = jnp.concatenate([oa.reshape(bn, t, A_WIDTH), ob], axis=-1).astype(x.dtype)
    x = x + mix @ w_out
    hm = rmsnorm(x, mlp_norm_g)
    u = jnp.square(jax.nn.relu(hm @ w_up))
    return x + u @ w_down


def setup_inputs(seed: int = 0) -> dict:
    key = jax.random.key(seed)
    ks = jax.random.split(key, 16)
    lb = min(A_BUF, PAST_LEN)
    nrm = jax.random.normal
    f32 = jnp.float32
    return {
        "x_prompt": nrm(ks[0], (BATCH, SEQ, D_MODEL), f32),
        "x_sample": nrm(ks[1], (DEC_BATCH, DEC_SEQ, D_MODEL), f32),
        "cache_win_k": nrm(ks[2], (DEPTH, DEC_BATCH, lb, A_HEADS, HEAD_DIM), f32),
        "cache_win_v": nrm(ks[3], (DEPTH, DEC_BATCH, lb, A_HEADS, HEAD_DIM), f32),
        "state_gla": nrm(ks[4], (DEPTH, DEC_BATCH, B_HEADS, B_DK, B_DV), f32),
        "attn_norm_g": 1.0 + 0.02 * nrm(ks[5], (DEPTH, D_MODEL), f32),
        "w_in": nrm(ks[6], (DEPTH, D_MODEL, N_IN), f32) * D_MODEL ** -0.5,
        "q_norm_g": 1.0 + 0.02 * nrm(ks[7], (DEPTH, HEAD_DIM), f32),
        "k_norm_g": 1.0 + 0.02 * nrm(ks[8], (DEPTH, HEAD_DIM), f32),
        "w_gate2": nrm(ks[9], (DEPTH, GATE_RANK, B_HEADS * B_DK), f32) * GATE_RANK ** -0.5,
        "b_gate": 0.1 * nrm(ks[10], (DEPTH, B_HEADS * B_DK), f32),
        "gla_norm_g": 1.0 + 0.02 * nrm(ks[11], (DEPTH, B_HEADS, B_DV), f32),
        "w_out": nrm(ks[12], (DEPTH, MIX_WIDTH, D_MODEL), f32) * MIX_WIDTH ** -0.5,
        "mlp_norm_g": 1.0 + 0.02 * nrm(ks[13], (DEPTH, D_MODEL), f32),
        "w_up": nrm(ks[14], (DEPTH, D_MODEL, D_FF), f32) * D_MODEL ** -0.5,
        "w_down": nrm(ks[15], (DEPTH, D_FF, D_MODEL), f32) * D_FF ** -0.5,
    }


def reference(x_prompt, x_sample, cache_win_k, cache_win_v, state_gla, attn_norm_g, w_in,
              q_norm_g, k_norm_g, w_gate2, b_gate, gla_norm_g, w_out, mlp_norm_g, w_up, w_down):
    xp, xs = x_prompt, x_sample
    pk, pv, ps, sk, sv, ss = [], [], [], [], [], []
    for l in range(DEPTH):
        h = rmsnorm(xp, attn_norm_g[l])
        qa, ka, va, qb, kb, vb, rb, log_a = mixer_inputs(h, w_in[l], q_norm_g[l], k_norm_g[l],
                                                          w_gate2[l], b_gate[l])
        outs, lses = [], []
        for window, dilation in DILATED_CONFIGS:
            o_i, l_i = dilated_window_full(qa, ka, va, window, dilation)
            outs.append(o_i)
            lses.append(l_i)
        oa = combine_by_denominator(outs, lses)
        s0 = jnp.zeros((xp.shape[0], B_HEADS, B_DK, B_DV), jnp.float32)
        ob, s_fin = gla_chunked(qb, kb, vb, log_a, s0)
        t = xp.shape[1]
        lp = min(A_BUF, t)
        pk.append(ka[:, t - lp:].astype(cache_win_k.dtype))
        pv.append(va[:, t - lp:].astype(cache_win_v.dtype))
        ps.append(s_fin.astype(state_gla.dtype))
        xp = layer_output(xp, oa, ob, rb, gla_norm_g[l], w_out[l], mlp_norm_g[l], w_up[l], w_down[l])

        h = rmsnorm(xs, attn_norm_g[l])
        qa, ka, va, qb, kb, vb, rb, log_a = mixer_inputs(h, w_in[l], q_norm_g[l], k_norm_g[l],
                                                          w_gate2[l], b_gate[l])
        k_ext = jnp.concatenate([cache_win_k[l].astype(jnp.float32), ka], axis=1)
        v_ext = jnp.concatenate([cache_win_v[l].astype(jnp.float32), va], axis=1)
        outs, lses = [], []
        for window, dilation in DILATED_CONFIGS:
            o_i, l_i = dilated_window_step(qa, k_ext, v_ext, window, dilation)
            outs.append(o_i)
            lses.append(l_i)
        oa = combine_by_denominator(outs, lses)
        ob, s_new = gla_chunked(qb, kb, vb, log_a, state_gla[l].astype(jnp.float32))
        sk.append(ka.astype(cache_win_k.dtype))
        sv.append(va.astype(cache_win_v.dtype))
        ss.append(s_new.astype(state_gla.dtype))
        xs = layer_output(xs, oa, ob, rb, gla_norm_g[l], w_out[l], mlp_norm_g[l], w_up[l], w_down[l])

    return (xp, xs, jnp.stack(pk), jnp.stack(pv), jnp.stack(ps), jnp.stack(sk), jnp.stack(sv), jnp.stack(ss))
```

```python
import functools

import jax
import jax.numpy as jnp
from jax import lax
from jax.experimental import pallas as pl
from jax.experimental.pallas import tpu as pltpu

F32 = jnp.float32
BF16 = jnp.bfloat16

HEAD_DIM = 128
A_HEADS = 8
A_WIDTH = A_HEADS * HEAD_DIM
DILATIONS = (1, 4, 16)
KEYS_BACK = 128
A_BUF = KEYS_BACK * max(DILATIONS)
B_HEADS = 4
B_DK = 128
B_DV = 256
B_WIDTH = B_HEADS * B_DV
GATE_RANK = 16
GATE_TAU = 16.0
GLA_CHUNK = 64
RMS_EPS = 1e-6
LANES = 128
NEG_INF = float("-inf")

VMEM_LIMIT_BYTES = 56 * 1024 * 1024


def _params(semantics):
    return pltpu.CompilerParams(dimension_semantics=semantics, vmem_limit_bytes=VMEM_LIMIT_BYTES)


def _dot(a, b):
    return jnp.dot(a, b, preferred_element_type=F32)


def _dot_nt(a, b):
    return lax.dot_general(a, b, (((1,), (1,)), ((), ())), preferred_element_type=F32)


def _dot_tn(a, b):
    return lax.dot_general(a, b, (((0,), (0,)), ((), ())), preferred_element_type=F32)


def _rms_scale(x, g):
    return x * lax.rsqrt(jnp.mean(x * x, axis=-1, keepdims=True) + RMS_EPS) * g


N_MAIN_TILES = 6


def _inproj_kernel(x_ref, g_ref, w_ref, wga_ref, wg2_ref, bg_ref, qg_ref, kg_ref,
                   qa_ref, ka_ref, va_ref, kwin_ref, vwin_ref, qkb_ref, vb_ref, rb_ref, la_ref, hn_ref):
    j = pl.program_id(1)
    tm = x_ref.shape[0]

    @pl.when(j == 0)
    def _():
        hn_ref[...] = _rms_scale(x_ref[...], g_ref[...]).astype(BF16)

    def per_head(y, fn, out_ref, win_ref):
        for h in range(A_HEADS):
            sl = slice(h * HEAD_DIM, (h + 1) * HEAD_DIM)
            yh = fn(y[:, sl])
            out_ref[:, sl] = yh
            if win_ref is not None:
                win_ref[pl.ds(h, tm, stride=A_HEADS), :] = yh

    @pl.when(j == 0)
    def _():
        per_head(_dot(hn_ref[...], w_ref[...]), lambda y: _rms_scale(y, qg_ref[...]), qa_ref, None)

    @pl.when(j == 1)
    def _():
        per_head(_dot(hn_ref[...], w_ref[...]), lambda y: _rms_scale(y, kg_ref[...]), ka_ref, kwin_ref)

    @pl.when(j == 2)
    def _():
        per_head(_dot(hn_ref[...], w_ref[...]), lambda y: y, va_ref, vwin_ref)

    @pl.when(j == 3)
    def _():
        y = _dot(hn_ref[...], w_ref[...])
        half = B_HEADS * B_DK
        qkb_ref[:, :half] = y[:, :half] * (B_DK ** -0.5)
        qkb_ref[:, half:] = y[:, half:]
        ga = _dot(hn_ref[...], wga_ref[...])
        z = _dot(ga.astype(BF16), wg2_ref[...]) + bg_ref[...]
        log_sig = jnp.minimum(z, 0.0) - jnp.log1p(jnp.exp(-jnp.abs(z)))
        la_ref[...] = log_sig * (1.0 / GATE_TAU)

    @pl.when(j == 4)
    def _():
        vb_ref[...] = _dot(hn_ref[...], w_ref[...]).astype(BF16)

    @pl.when(j == 5)
    def _():
        rb_ref[...] = _dot(hn_ref[...], w_ref[...])


def _inproj(x, g, w_main, w_ga, w_g2, b_gate, q_g, k_g, tm):
    t, d = x.shape
    tn = A_WIDTH
    row = lambda i, j: (i, 0)
    const = lambda i, j: (0, 0)
    out_shapes = (
        jax.ShapeDtypeStruct((t, A_WIDTH), F32),
        jax.ShapeDtypeStruct((t, A_WIDTH), F32),
        jax.ShapeDtypeStruct((t, A_WIDTH), F32),
        jax.ShapeDtypeStruct((t * A_HEADS, HEAD_DIM), F32),
        jax.ShapeDtypeStruct((t * A_HEADS, HEAD_DIM), F32),
        jax.ShapeDtypeStruct((t, 2 * B_HEADS * B_DK), F32),
        jax.ShapeDtypeStruct((t, B_WIDTH), BF16),
        jax.ShapeDtypeStruct((t, B_WIDTH), F32),
        jax.ShapeDtypeStruct((t, B_HEADS * B_DK), F32),
    )
    return pl.pallas_call(
        _inproj_kernel,
        grid=(t // tm, N_MAIN_TILES),
        in_specs=[
            pl.BlockSpec((tm, d), row),
            pl.BlockSpec((1, d), const),
            pl.BlockSpec((d, tn), lambda i, j: (0, j)),
            pl.BlockSpec((d, LANES), const),
            pl.BlockSpec((LANES, B_HEADS * B_DK), const),
            pl.BlockSpec((1, B_HEADS * B_DK), const),
            pl.BlockSpec((1, HEAD_DIM), const),
            pl.BlockSpec((1, HEAD_DIM), const),
        ],
        out_specs=tuple(pl.BlockSpec((tm * s.shape[0] // t, s.shape[1]), row) for s in out_shapes),
        out_shape=out_shapes,
        scratch_shapes=[pltpu.VMEM((tm, d), BF16)],
        compiler_params=_params(("parallel", "arbitrary")),
        name="inproj",
    )(x, g, w_main, w_ga, w_g2, b_gate, q_g, k_g)


BLK = KEYS_BACK


def _softmax_pv(s, valid, v):
    s = jnp.where(valid, s, NEG_INF)
    m = jnp.max(s, axis=-1, keepdims=True)
    p = jnp.exp(s - m)
    l = jnp.sum(p, axis=-1, keepdims=True)
    o = _dot(p.astype(BF16), v) / l
    return o, m + jnp.log(l)


def _first_block(qb_ref, kb_ref, vb_ref, start):
    q = qb_ref[pl.ds(start, BLK), :]
    k = kb_ref[pl.ds(start, BLK), :]
    v = vb_ref[pl.ds(start, BLK), :]
    s = _dot_nt(q, k) * (HEAD_DIM ** -0.5)
    qi = lax.broadcasted_iota(jnp.int32, (BLK, BLK), 0)
    kj = lax.broadcasted_iota(jnp.int32, (BLK, BLK), 1)
    return _softmax_pv(s, kj <= qi, v)


def _band_block(qb_ref, kb_ref, vb_ref, start):
    q = qb_ref[pl.ds(start, BLK), :]
    k = kb_ref[pl.ds(start - BLK, 2 * BLK), :]
    v = vb_ref[pl.ds(start - BLK, 2 * BLK), :]
    s = _dot_nt(q, k) * (HEAD_DIM ** -0.5)
    qi = lax.broadcasted_iota(jnp.int32, (BLK, 2 * BLK), 0)
    kj = lax.broadcasted_iota(jnp.int32, (BLK, 2 * BLK), 1)
    return _softmax_pv(s, (kj >= qi) & (kj <= qi + BLK), v)


def _attn_prompt_kernel(q_ref, k_ref, v_ref, o_ref,
                        qs_ref, ks_ref, vs_ref, o1_ref, l1_ref, o4_ref, l4_ref, o16_ref, l16_ref,
                        *, seq):
    def store(o_out, l_out, pos, stride, o, lse):
        if stride == 1:
            idx = pl.ds(pos, BLK)
        else:
            idx = pl.ds(pos, BLK, stride=stride)
        o_out[idx, :] = o
        l_out[idx, :] = jnp.broadcast_to(lse, (BLK, LANES))

    for d, o_out, l_out in ((1, o1_ref, l1_ref), (4, o4_ref, l4_ref), (16, o16_ref, l16_ref)):
        sub = seq // d
        nblk = sub // BLK
        for r in range(d):
            idx = pl.ds(r, sub) if d == 1 else pl.ds(r, sub, stride=d)
            qs_ref[r * sub:(r + 1) * sub, :] = q_ref[idx, :].astype(BF16)
            ks_ref[r * sub:(r + 1) * sub, :] = k_ref[idx, :].astype(BF16)
            vs_ref[r * sub:(r + 1) * sub, :] = v_ref[idx, :].astype(BF16)

        def first_body(r, carry, d=d, sub=sub, o_out=o_out, l_out=l_out):
            o, lse = _first_block(qs_ref, ks_ref, vs_ref, pl.multiple_of(r * sub, BLK))
            store(o_out, l_out, r, d, o, lse)
            return carry

        lax.fori_loop(0, d, first_body, 0)

        if nblk > 1:
            def band_body(i, carry, d=d, sub=sub, nblk=nblk, o_out=o_out, l_out=l_out):
                r = i // (nblk - 1)
                n = i % (nblk - 1) + 1
                o, lse = _band_block(qs_ref, ks_ref, vs_ref, pl.multiple_of(r * sub + n * BLK, BLK))
                store(o_out, l_out, r + n * BLK * d, d, o, lse)
                return carry

            lax.fori_loop(0, d * (nblk - 1), band_body, 0)

    rows = 256

    def merge_body(c, carry):
        sl = pl.ds(pl.multiple_of(c * rows, rows), rows)
        l1, l4, l16 = l1_ref[sl, :], l4_ref[sl, :], l16_ref[sl, :]
        m = jnp.maximum(jnp.maximum(l1, l4), l16)
        e1, e4, e16 = jnp.exp(l1 - m), jnp.exp(l4 - m), jnp.exp(l16 - m)
        num = e1 * o1_ref[sl, :] + e4 * o4_ref[sl, :] + e16 * o16_ref[sl, :]
        o_ref[sl, :] = (num / (e1 + e4 + e16)).astype(o_ref.dtype)
        return carry

    lax.fori_loop(0, seq // rows, merge_body, 0)


def _attn_prompt(qa, ka, va):
    bn, seq, _ = qa.shape
    assert seq == A_BUF, "prompt attention is written for a sequence of exactly the largest window"
    spec = pl.BlockSpec((None, seq, HEAD_DIM), lambda b, h: (b, 0, h))
    return pl.pallas_call(
        functools.partial(_attn_prompt_kernel, seq=seq),
        grid=(bn, A_HEADS),
        in_specs=[spec, spec, spec],
        out_specs=spec,
        out_shape=jax.ShapeDtypeStruct((bn, seq, A_WIDTH), BF16),
        scratch_shapes=[pltpu.VMEM((seq, HEAD_DIM), BF16)] * 3 + [pltpu.VMEM((seq, LANES), F32)] * 6,
        compiler_params=_params(("parallel", "parallel")),
        name="attn_prompt",
    )(qa, ka, va)


Q_ROWS = 8
NEAR = 4 * KEYS_BACK


def _attn_sample_head(h, q, kn, vn, ck_ref, cv_ref, kx_ref, vx_ref, k16_ref, v16_ref, *, cache_len, n_new):
    scale = HEAD_DIM ** -0.5
    qb = q.astype(BF16)
    ext = NEAR + LANES

    def cache_rows(ref, first_pos, n, pos_stride):
        return ref[pl.ds(first_pos * A_HEADS + h, n, stride=pos_stride * A_HEADS), :].astype(BF16)

    kx_ref[:NEAR, :] = cache_rows(ck_ref, cache_len - NEAR, NEAR, 1)
    vx_ref[:NEAR, :] = cache_rows(cv_ref, cache_len - NEAR, NEAR, 1)
    kx_ref[NEAR:, :] = jnp.zeros((LANES, HEAD_DIM), BF16)
    vx_ref[NEAR:, :] = jnp.zeros((LANES, HEAD_DIM), BF16)
    kx_ref[NEAR:NEAR + 16, :] = jnp.concatenate([kn, jnp.zeros_like(kn)], axis=0).astype(BF16)
    vx_ref[NEAR:NEAR + 16, :] = jnp.concatenate([vn, jnp.zeros_like(vn)], axis=0).astype(BF16)

    def softmax_parts(s, valid, s_self=None):
        s = jnp.where(valid, s, NEG_INF)
        m = jnp.max(s, axis=-1, keepdims=True)
        if s_self is not None:
            m = jnp.maximum(m, s_self)
        p = jnp.exp(s - m)
        l = jnp.sum(p, axis=-1, keepdims=True)
        return p, m, l

    def trow(shape):
        return lax.broadcasted_iota(jnp.int32, shape, 0) & 3

    w1 = 2 * KEYS_BACK
    k1 = kx_ref[NEAR - KEYS_BACK:NEAR + KEYS_BACK, :]
    v1 = vx_ref[NEAR - KEYS_BACK:NEAR + KEYS_BACK, :]
    t = trow((Q_ROWS, w1))
    c = lax.broadcasted_iota(jnp.int32, (Q_ROWS, w1), 1)
    p, m1, l1 = softmax_parts(_dot_nt(qb, k1) * scale, (c >= t) & (c <= KEYS_BACK + t))
    o1 = _dot(p.astype(BF16), v1) / l1
    lse1 = m1 + jnp.log(l1)

    t = trow((Q_ROWS, ext))
    c = lax.broadcasted_iota(jnp.int32, (Q_ROWS, ext), 1)
    p, m4, l4 = softmax_parts(_dot_nt(qb, kx_ref[...]) * scale, ((c & 3) == t) & (c <= NEAR + t))
    o4 = _dot(p.astype(BF16), vx_ref[...]) / l4
    lse4 = m4 + jnp.log(l4)

    for tt in range(n_new):
        k16_ref[tt * KEYS_BACK:(tt + 1) * KEYS_BACK, :] = cache_rows(ck_ref, tt, KEYS_BACK, 16)
        v16_ref[tt * KEYS_BACK:(tt + 1) * KEYS_BACK, :] = cache_rows(cv_ref, tt, KEYS_BACK, 16)
    w16 = n_new * KEYS_BACK
    t = trow((Q_ROWS, w16))
    c = lax.broadcasted_iota(jnp.int32, (Q_ROWS, w16), 1)
    s_self = jnp.sum(qb.astype(F32) * kn.astype(BF16).astype(F32), axis=-1, keepdims=True) * scale
    p, m16, l16 = softmax_parts(_dot_nt(qb, k16_ref[...]) * scale, (c >> 7) == t, s_self)
    p_self = jnp.exp(s_self - m16)
    l16 = l16 + p_self
    o16 = (_dot(p.astype(BF16), v16_ref[...]) + p_self.astype(BF16).astype(F32) * vn.astype(BF16).astype(F32)) / l16
    lse16 = m16 + jnp.log(l16)

    m = jnp.maximum(jnp.maximum(lse1, lse4), lse16)
    e1, e4, e16 = jnp.exp(lse1 - m), jnp.exp(lse4 - m), jnp.exp(lse16 - m)
    return (e1 * o1 + e4 * o4 + e16 * o16) / (e1 + e4 + e16)


def _attn_sample_kernel(q_ref, kn_ref, vn_ref, ck_ref, cv_ref, o_ref, kx_ref, vx_ref, k16_ref, v16_ref,
                        *, cache_len, n_new):
    for h in range(A_HEADS):
        sl = slice(h * HEAD_DIM, (h + 1) * HEAD_DIM)
        o_ref[:, sl] = _attn_sample_head(h, q_ref[:, sl], kn_ref[:, sl], vn_ref[:, sl], ck_ref, cv_ref,
                                         kx_ref, vx_ref, k16_ref, v16_ref, cache_len=cache_len, n_new=n_new)


def _attn_sample(qn, kn, vn, cache_k, cache_v, n_new):
    bn, rows, _ = cache_k.shape
    cache_len = rows // A_HEADS
    assert KEYS_BACK == 128 and cache_len == A_BUF and n_new == 4, "written for a full window cache and 4 new tokens"
    new_spec = pl.BlockSpec((None, Q_ROWS, A_WIDTH), lambda b: (b, 0, 0))
    cache_spec = pl.BlockSpec((None, rows, HEAD_DIM), lambda b: (b, 0, 0))
    ext = NEAR + LANES
    return pl.pallas_call(
        functools.partial(_attn_sample_kernel, cache_len=cache_len, n_new=n_new),
        grid=(bn,),
        in_specs=[new_spec, new_spec, new_spec, cache_spec, cache_spec],
        out_specs=new_spec,
        out_shape=jax.ShapeDtypeStruct((bn, Q_ROWS, A_WIDTH), F32),
        scratch_shapes=[pltpu.VMEM((ext, HEAD_DIM), BF16)] * 2 + [pltpu.VMEM((n_new * KEYS_BACK, HEAD_DIM), BF16)] * 2,
        compiler_params=_params(("parallel",)),
        name="attn_sample",
    )(qn, kn, vn, cache_k, cache_v)


def _gla_kernel(q_ref, k_ref, la_ref, v_ref, rb_ref, g_ref, s0_ref, o_ref, sfin_ref, st_ref, *, n_chunks):
    c = GLA_CHUNK
    st_ref[...] = s0_ref[...].T
    ti = lax.broadcasted_iota(jnp.int32, (c, c), 0)
    tj = lax.broadcasted_iota(jnp.int32, (c, c), 1)
    causal = tj <= ti
    tri = causal.astype(F32)

    def chunk_body(n, carry):
        rows = pl.ds(pl.multiple_of(n * c, c), c)
        la = la_ref[rows, :]
        b = jnp.dot(tri, la, preferred_element_type=F32, precision=lax.Precision.HIGHEST)
        b_last = b[c - 1:c, :]
        q = q_ref[rows, :]
        k = k_ref[rows, :]
        v = v_ref[rows, :]
        q_dec = (q * jnp.exp(b)).astype(BF16)
        k_inv = (k * jnp.exp(-b)).astype(BF16)
        k_end = (k * jnp.exp(b_last - b)).astype(BF16)
        att = jnp.where(causal, _dot_nt(q_dec, k_inv), 0.0)
        st = st_ref[...]
        o = _dot(att.astype(BF16), v) + _dot_nt(q_dec, st.astype(BF16))
        st_ref[...] = st * jnp.exp(b_last) + _dot_tn(v, k_end)
        gated = _rms_scale(o, g_ref[...]) * jax.nn.silu(rb_ref[rows, :])
        o_ref[rows, :] = gated.astype(o_ref.dtype)
        return carry

    lax.fori_loop(0, n_chunks, chunk_body, 0)
    sfin_ref[...] = st_ref[...].T


def _gla(qkb, la, vb, rb, g, s0, out_dtype):
    bn, t, _ = qkb.shape
    assert t % GLA_CHUNK == 0
    return pl.pallas_call(
        functools.partial(_gla_kernel, n_chunks=t // GLA_CHUNK),
        grid=(bn, B_HEADS),
        in_specs=[
            pl.BlockSpec((None, t, B_DK), lambda b, h: (b, 0, h)),
            pl.BlockSpec((None, t, B_DK), lambda b, h: (b, 0, B_HEADS + h)),
            pl.BlockSpec((None, t, B_DK), lambda b, h: (b, 0, h)),
            pl.BlockSpec((None, t, B_DV), lambda b, h: (b, 0, h)),
            pl.BlockSpec((None, t, B_DV), lambda b, h: (b, 0, h)),
            pl.BlockSpec((None, 1, B_DV), lambda b, h: (h, 0, 0)),
            pl.BlockSpec((None, None, B_DK, B_DV), lambda b, h: (b, h, 0, 0)),
        ],
        out_specs=(
            pl.BlockSpec((None, t, B_DV), lambda b, h: (b, 0, h)),
            pl.BlockSpec((None, None, B_DK, B_DV), lambda b, h: (b, h, 0, 0)),
        ),
        out_shape=(
            jax.ShapeDtypeStruct((bn, t, B_WIDTH), out_dtype),
            jax.ShapeDtypeStruct((bn, B_HEADS, B_DK, B_DV), F32),
        ),
        scratch_shapes=[pltpu.VMEM((B_DV, B_DK), F32)],
        compiler_params=_params(("parallel", "parallel")),
        name="gla",
    )(qkb, qkb, la, vb, rb, g, s0)


def _outproj_kernel(x_ref, ma_ref, mb_ref, wa_ref, wb_ref, g_ref, x1_ref, hm_ref):
    x1 = x_ref[...] + _dot(ma_ref[...].astype(BF16), wa_ref[...]) + _dot(mb_ref[...].astype(BF16), wb_ref[...])
    x1_ref[...] = x1
    hm_ref[...] = _rms_scale(x1, g_ref[...]).astype(BF16)


def _outproj(x, mix_a, mix_b, w_a, w_b, g, tm):
    t, d = x.shape
    row = lambda i: (i, 0)
    const = lambda i: (0, 0)
    return pl.pallas_call(
        _outproj_kernel,
        grid=(t // tm,),
        in_specs=[
            pl.BlockSpec((tm, d), row),
            pl.BlockSpec((tm, A_WIDTH), row),
            pl.BlockSpec((tm, B_WIDTH), row),
            pl.BlockSpec((A_WIDTH, d), const),
            pl.BlockSpec((B_WIDTH, d), const),
            pl.BlockSpec((1, d), const),
        ],
        out_specs=(pl.BlockSpec((tm, d), row), pl.BlockSpec((tm, d), row)),
        out_shape=(jax.ShapeDtypeStruct((t, d), F32), jax.ShapeDtypeStruct((t, d), BF16)),
        compiler_params=_params(("parallel",)),
        name="outproj",
    )(x, mix_a, mix_b, w_a, w_b, g)


def _mlp_kernel(x1_ref, hm_ref, wu_ref, wd_ref, y_ref):
    f = pl.program_id(1)

    @pl.when(f == 0)
    def _():
        y_ref[...] = x1_ref[...]

    u = jnp.square(jnp.maximum(_dot(hm_ref[...], wu_ref[...]), 0.0))
    y_ref[...] += _dot(u.astype(BF16), wd_ref[...])


def _mlp(x1, hm, w_up, w_down, tm, tf):
    t, d = x1.shape
    d_ff = w_up.shape[1]
    row = lambda i, f: (i, 0)
    return pl.pallas_call(
        _mlp_kernel,
        grid=(t // tm, d_ff // tf),
        in_specs=[
            pl.BlockSpec((tm, d), row),
            pl.BlockSpec((tm, d), row),
            pl.BlockSpec((d, tf), lambda i, f: (0, f)),
            pl.BlockSpec((tf, d), lambda i, f: (f, 0)),
        ],
        out_specs=pl.BlockSpec((tm, d), row),
        out_shape=jax.ShapeDtypeStruct((t, d), F32),
        compiler_params=_params(("parallel", "arbitrary")),
        name="mlp",
    )(x1, hm, w_up, w_down)


def _row_tile(t):
    return 512 if t % 512 == 0 else t


def _layer_weights(attn_norm_g, w_in, q_norm_g, k_norm_g, w_gate2, b_gate, gla_norm_g, w_out, mlp_norm_g,
                   w_up, w_down):
    n_main = N_MAIN_TILES * A_WIDTH
    w_ga = jnp.pad(w_in[:, n_main:], ((0, 0), (0, LANES - GATE_RANK)))
    w_g2 = jnp.pad(w_gate2, ((0, LANES - GATE_RANK), (0, 0)))
    return dict(
        attn_g=attn_norm_g[None, :],
        w_main=w_in[:, :n_main].astype(BF16),
        w_ga=w_ga.astype(BF16),
        w_g2=w_g2.astype(BF16),
        b_gate=b_gate[None, :],
        q_g=q_norm_g[None, :],
        k_g=k_norm_g[None, :],
        gla_g=gla_norm_g[:, None, :],
        w_out_a=w_out[:A_WIDTH].astype(BF16),
        w_out_b=w_out[A_WIDTH:].astype(BF16),
        mlp_g=mlp_norm_g[None, :],
        w_up=w_up.astype(BF16),
        w_down=w_down.astype(BF16),
    )


def _mixer_inputs(x2d, w):
    return _inproj(x2d, w["attn_g"], w["w_main"], w["w_ga"], w["w_g2"], w["b_gate"], w["q_g"], w["k_g"],
                   _row_tile(x2d.shape[0]))


def _finish(x2d, mix_a, mix_b, w):
    tm = _row_tile(x2d.shape[0])
    x1, hm = _outproj(x2d, mix_a, mix_b, w["w_out_a"], w["w_out_b"], w["mlp_g"], tm)
    return _mlp(x1, hm, w["w_up"], w["w_down"], tm, 1024)


def _prompt_layer(xp, w):
    bn, seq, d = xp.shape
    x2d = xp.reshape(bn * seq, d)
    qa, ka, va, kwin, vwin, qkb, vb, rb, la = _mixer_inputs(x2d, w)
    b3 = lambda a: a.reshape(bn, seq, a.shape[-1])
    oa = _attn_prompt(b3(qa), b3(ka), b3(va))
    s0 = jnp.zeros((bn, B_HEADS, B_DK, B_DV), F32)
    ob, s_fin = _gla(b3(qkb), b3(la), b3(vb), b3(rb), w["gla_g"], s0, BF16)
    y = _finish(x2d, oa.reshape(bn * seq, A_WIDTH), ob.reshape(bn * seq, B_WIDTH), w)
    lp = min(A_BUF, seq)
    win_k = kwin.reshape(bn, seq, A_HEADS, HEAD_DIM)[:, seq - lp:]
    win_v = vwin.reshape(bn, seq, A_HEADS, HEAD_DIM)[:, seq - lp:]
    return y.reshape(bn, seq, d), win_k, win_v, s_fin


def _sample_layer(xs, cache_k, cache_v, state, w):
    bn, n_new, d = xs.shape
    x2d = xs.reshape(bn * n_new, d)
    qa, ka, va, kwin, vwin, qkb, vb, rb, la = _mixer_inputs(x2d, w)

    def padded(a, rows):
        a = a.reshape(bn, n_new, a.shape[-1])
        return jnp.pad(a, ((0, 0), (0, rows - n_new), (0, 0)))

    cache_len = cache_k.shape[1]
    oa = _attn_sample(padded(qa, Q_ROWS), padded(ka, Q_ROWS), padded(va, Q_ROWS),
                      cache_k.reshape(bn, cache_len * A_HEADS, HEAD_DIM),
                      cache_v.reshape(bn, cache_len * A_HEADS, HEAD_DIM), n_new)
    c = GLA_CHUNK
    ob, s_new = _gla(padded(qkb, c), padded(la, c), padded(vb, c), padded(rb, c), w["gla_g"], state, F32)
    y = _finish(x2d, oa[:, :n_new].reshape(bn * n_new, A_WIDTH), ob[:, :n_new].reshape(bn * n_new, B_WIDTH), w)
    win_k = kwin.reshape(bn, n_new, A_HEADS, HEAD_DIM)
    win_v = vwin.reshape(bn, n_new, A_HEADS, HEAD_DIM)
    return y.reshape(bn, n_new, d), win_k, win_v, s_new


def kernel(x_prompt, x_sample, cache_win_k, cache_win_v, state_gla, attn_norm_g, w_in, q_norm_g, k_norm_g,
           w_gate2, b_gate, gla_norm_g, w_out, mlp_norm_g, w_up, w_down):
    depth = w_in.shape[0]
    xp, xs = x_prompt, x_sample
    pk, pv, ps, sk, sv, ss = [], [], [], [], [], []
    for l in range(depth):
        w = _layer_weights(attn_norm_g[l], w_in[l], q_norm_g[l], k_norm_g[l], w_gate2[l], b_gate[l],
                           gla_norm_g[l], w_out[l], mlp_norm_g[l], w_up[l], w_down[l])
        xp, k_l, v_l, s_l = _prompt_layer(xp, w)
        pk.append(k_l)
        pv.append(v_l)
        ps.append(s_l)
        xs, k_l, v_l, s_l = _sample_layer(xs, cache_win_k[l], cache_win_v[l], state_gla[l], w)
        sk.append(k_l)
        sv.append(v_l)
        ss.append(s_l)
    return (xp, xs, jnp.stack(pk), jnp.stack(pv), jnp.stack(ps), jnp.stack(sk), jnp.stack(sv), jnp.stack(ss))
```

```python
import functools

import jax
import jax.numpy as jnp
from jax import lax
from jax.experimental import pallas as pl
from jax.experimental.pallas import tpu as pltpu

F32 = jnp.float32
BF16 = jnp.bfloat16

HEAD_DIM = 128
A_HEADS = 8
A_WIDTH = A_HEADS * HEAD_DIM
DILATIONS = (1, 4, 16)
KEYS_BACK = 128
A_BUF = KEYS_BACK * max(DILATIONS)
B_HEADS = 4
B_DK = 128
B_DV = 256
B_WIDTH = B_HEADS * B_DV
GATE_RANK = 16
GATE_TAU = 16.0
GLA_CHUNK = 64
RMS_EPS = 1e-6
LANES = 128
NEG_INF = float("-inf")

VMEM_LIMIT_BYTES = 56 * 1024 * 1024


def _params(semantics):
    return pltpu.CompilerParams(dimension_semantics=semantics, vmem_limit_bytes=VMEM_LIMIT_BYTES)


def _dot(a, b):
    return jnp.dot(a, b, preferred_element_type=F32)


def _dot_nt(a, b):
    return lax.dot_general(a, b, (((1,), (1,)), ((), ())), preferred_element_type=F32)


def _dot_tn(a, b):
    return lax.dot_general(a, b, (((0,), (0,)), ((), ())), preferred_element_type=F32)


def _rms_scale(x, g):
    return x * lax.rsqrt(jnp.mean(x * x, axis=-1, keepdims=True) + RMS_EPS) * g


N_MAIN_TILES = 6


CUMSUM_ROWS = 128


def _inproj_kernel(x_ref, g_ref, w_ref, wga_ref, wg2_ref, bg_ref, qg_ref, kg_ref,
                   qa_ref, ka_ref, va_ref, kwin_ref, vwin_ref, qkb_ref, vb_ref, rb_ref, lb_ref, hn_ref,
                   *, chunk):
    tm = x_ref.shape[0]
    hn_ref[...] = _rms_scale(x_ref[...], g_ref[...]).astype(BF16)

    def proj(j):
        return _dot(hn_ref[...], w_ref[:, j * A_WIDTH:(j + 1) * A_WIDTH])

    def per_head(y, fn, out_ref, win_ref):
        for h in range(A_HEADS):
            sl = slice(h * HEAD_DIM, (h + 1) * HEAD_DIM)
            yh = fn(y[:, sl])
            out_ref[:, sl] = yh
            if win_ref is not None:
                win_ref[pl.ds(h, tm, stride=A_HEADS), :] = yh

    ga = _dot(hn_ref[...], wga_ref[...])
    z = _dot(ga.astype(BF16), wg2_ref[...]) + bg_ref[...]
    log_a = (jnp.minimum(z, 0.0) - jnp.log1p(jnp.exp(-jnp.abs(z)))) * (1.0 / GATE_TAU)

    per_head(proj(0), lambda y: _rms_scale(y, qg_ref[...]), qa_ref, None)

    shift = chunk.bit_length() - 1
    ti = lax.broadcasted_iota(jnp.int32, (CUMSUM_ROWS, CUMSUM_ROWS), 0)
    tj = lax.broadcasted_iota(jnp.int32, (CUMSUM_ROWS, CUMSUM_ROWS), 1)
    tri = (((ti >> shift) == (tj >> shift)) & (tj <= ti)).astype(F32)
    for r in range(tm // CUMSUM_ROWS):
        rows = slice(r * CUMSUM_ROWS, (r + 1) * CUMSUM_ROWS)
        lb_ref[rows, :] = jnp.dot(tri, log_a[rows, :], preferred_element_type=F32,
                                  precision=lax.Precision.HIGHEST)

    per_head(proj(1), lambda y: _rms_scale(y, kg_ref[...]), ka_ref, kwin_ref)
    per_head(proj(2), lambda y: y, va_ref, vwin_ref)

    y = proj(3)
    half = B_HEADS * B_DK
    qkb_ref[:, :half] = y[:, :half] * (B_DK ** -0.5)
    qkb_ref[:, half:] = y[:, half:]

    vb_ref[...] = proj(4).astype(BF16)
    rb_ref[...] = proj(5)


def _inproj(x, g, w_main, w_ga, w_g2, b_gate, q_g, k_g, tm, chunk):
    t, d = x.shape
    assert chunk & (chunk - 1) == 0 and CUMSUM_ROWS % chunk == 0 and tm % CUMSUM_ROWS == 0
    row = lambda i: (i, 0)
    const = lambda i: (0, 0)
    resident = dict(pipeline_mode=pl.Buffered(1))
    out_shapes = (
        jax.ShapeDtypeStruct((t, A_WIDTH), F32),
        jax.ShapeDtypeStruct((t, A_WIDTH), F32),
        jax.ShapeDtypeStruct((t, A_WIDTH), F32),
        jax.ShapeDtypeStruct((t * A_HEADS, HEAD_DIM), F32),
        jax.ShapeDtypeStruct((t * A_HEADS, HEAD_DIM), F32),
        jax.ShapeDtypeStruct((t, 2 * B_HEADS * B_DK), F32),
        jax.ShapeDtypeStruct((t, B_WIDTH), BF16),
        jax.ShapeDtypeStruct((t, B_WIDTH), F32),
        jax.ShapeDtypeStruct((t, B_HEADS * B_DK), F32),
    )
    return pl.pallas_call(
        functools.partial(_inproj_kernel, chunk=chunk),
        grid=(t // tm,),
        in_specs=[
            pl.BlockSpec((tm, d), row),
            pl.BlockSpec((1, d), const),
            pl.BlockSpec(w_main.shape, const, **resident),
            pl.BlockSpec((d, LANES), const, **resident),
            pl.BlockSpec((LANES, B_HEADS * B_DK), const),
            pl.BlockSpec((1, B_HEADS * B_DK), const),
            pl.BlockSpec((1, HEAD_DIM), const),
            pl.BlockSpec((1, HEAD_DIM), const),
        ],
        out_specs=tuple(pl.BlockSpec((tm * s.shape[0] // t, s.shape[1]), row) for s in out_shapes),
        out_shape=out_shapes,
        scratch_shapes=[pltpu.VMEM((tm, d), BF16)],
        compiler_params=_params(("parallel",)),
        name="inproj",
    )(x, g, w_main, w_ga, w_g2, b_gate, q_g, k_g)


BLK = KEYS_BACK


def _attend_blocks(blocks):
    scores = []
    for q_ref, k_ref, _, start, back in blocks:
        q = q_ref[pl.ds(start, BLK), :]
        k = k_ref[pl.ds(start - back, BLK + back), :]
        s = _dot_nt(q, k) * (HEAD_DIM ** -0.5)
        qi = lax.broadcasted_iota(jnp.int32, s.shape, 0)
        kj = lax.broadcasted_iota(jnp.int32, s.shape, 1)
        valid = (kj >= qi) & (kj <= qi + BLK) if back else kj <= qi
        scores.append(jnp.where(valid, s, NEG_INF))
    probs = []
    for s in scores:
        m = jnp.max(s, axis=-1, keepdims=True)
        p = jnp.exp(s - m)
        probs.append((p.astype(BF16), m, jnp.sum(p, axis=-1, keepdims=True)))
    outs = []
    for (p, m, l), (_, _, v_ref, start, back) in zip(probs, blocks):
        v = v_ref[pl.ds(start - back, BLK + back), :]
        outs.append((_dot(p, v) / l, m + jnp.log(l)))
    return outs


def _attn_prompt_kernel(q_ref, k_ref, v_ref, o_ref,
                        q1_ref, k1_ref, v1_ref, q4_ref, k4_ref, v4_ref, q16_ref, k16_ref, v16_ref,
                        q4f_ref, k4f_ref, v4f_ref,
                        o1_ref, l1_ref, o4_ref, l4_ref, o16_ref, l16_ref,
                        *, seq):
    def store(o_out, l_out, pos, stride, o, lse):
        if stride == 1:
            idx = pl.ds(pl.multiple_of(pos, BLK), BLK)
        else:
            idx = pl.ds(pos, BLK, stride=stride)
        o_out[idx, :] = o
        l_out[idx, :] = jnp.broadcast_to(lse, (BLK, LANES))

    sub4, sub16 = seq // 4, seq // 16
    for x_ref, x1_ref, x4_ref, x4f_ref, x16_ref in ((q_ref, q1_ref, q4_ref, q4f_ref, q16_ref),
                                                   (k_ref, k1_ref, k4_ref, k4f_ref, k16_ref),
                                                   (v_ref, v1_ref, v4_ref, v4f_ref, v16_ref)):
        x1_ref[...] = x_ref[...].astype(BF16)
        for r in range(4):
            x4 = x_ref[pl.ds(r, sub4, stride=4), :]
            x4f_ref[r * sub4:(r + 1) * sub4, :] = x4
            x4_ref[r * sub4:(r + 1) * sub4, :] = x4.astype(BF16)
        for r in range(16):
            x16 = x4f_ref[pl.ds((r % 4) * sub4 + r // 4, sub16, stride=4), :]
            x16_ref[r * sub16:(r + 1) * sub16, :] = x16.astype(BF16)

    d1 = (q1_ref, k1_ref, v1_ref)
    d4 = (q4_ref, k4_ref, v4_ref)
    d16 = (q16_ref, k16_ref, v16_ref)

    def run(blocks, dests):
        for (o, lse), (o_out, l_out, pos, stride) in zip(_attend_blocks(blocks), dests):
            store(o_out, l_out, pos, stride, o, lse)

    run([(*d1, 0, 0)] + [(*d4, r * sub4, 0) for r in range(4)],
        [(o1_ref, l1_ref, 0, 1)] + [(o4_ref, l4_ref, r, 4) for r in range(4)])

    group16 = 4

    def d16_body(i, carry):
        rs = [i * group16 + u for u in range(group16)]
        run([(*d16, pl.multiple_of(r * sub16, BLK), 0) for r in rs],
            [(o16_ref, l16_ref, r, 16) for r in rs])
        return carry

    lax.fori_loop(0, 16 // group16, d16_body, 0)

    group1 = 5

    def d1_body(i, carry):
        ns = [1 + i * group1 + u for u in range(group1)]
        run([(*d1, pl.multiple_of(n * BLK, BLK), BLK) for n in ns],
            [(o1_ref, l1_ref, n * BLK, 1) for n in ns])
        return carry

    lax.fori_loop(0, (seq // BLK - 1) // group1, d1_body, 0)

    def d4_body(n, carry):
        run([(*d4, pl.multiple_of(r * sub4 + n * BLK, BLK), BLK) for r in range(4)],
            [(o4_ref, l4_ref, r + n * BLK * 4, 4) for r in range(4)])
        return carry

    lax.fori_loop(1, sub4 // BLK, d4_body, 0)

    rows = 256

    def merge_body(c, carry):
        sl = pl.ds(pl.multiple_of(c * rows, rows), rows)
        l1, l4, l16 = l1_ref[sl, :], l4_ref[sl, :], l16_ref[sl, :]
        m = jnp.maximum(jnp.maximum(l1, l4), l16)
        e1, e4, e16 = jnp.exp(l1 - m), jnp.exp(l4 - m), jnp.exp(l16 - m)
        num = e1 * o1_ref[sl, :] + e4 * o4_ref[sl, :] + e16 * o16_ref[sl, :]
        o_ref[sl, :] = (num / (e1 + e4 + e16)).astype(o_ref.dtype)
        return carry

    lax.fori_loop(0, seq // rows, merge_body, 0)


def _attn_prompt(qa, ka, va):
    bn, seq, _ = qa.shape
    assert seq == A_BUF, "prompt attention is written for a sequence of exactly the largest window"
    spec = pl.BlockSpec((None, seq, HEAD_DIM), lambda b, h: (b, 0, h))
    return pl.pallas_call(
        functools.partial(_attn_prompt_kernel, seq=seq),
        grid=(bn, A_HEADS),
        in_specs=[spec, spec, spec],
        out_specs=spec,
        out_shape=jax.ShapeDtypeStruct((bn, seq, A_WIDTH), BF16),
        scratch_shapes=[pltpu.VMEM((seq, HEAD_DIM), BF16)] * 9 + [pltpu.VMEM((seq, LANES), F32)] * 9,
        compiler_params=_params(("parallel", "parallel")),
        name="attn_prompt",
    )(qa, ka, va)


Q_ROWS = 8
NEAR = 4 * KEYS_BACK


def _attn_sample_head(h, q, kn, vn, ck_ref, cv_ref, kx_ref, vx_ref, k16_ref, v16_ref, *, cache_len, n_new):
    scale = HEAD_DIM ** -0.5
    qb = q.astype(BF16)
    ext = NEAR + LANES

    def cache_rows(ref, first_pos, n, pos_stride):
        return ref[pl.ds(first_pos * A_HEADS + h, n, stride=pos_stride * A_HEADS), :].astype(BF16)

    kx_ref[:NEAR, :] = cache_rows(ck_ref, cache_len - NEAR, NEAR, 1)
    vx_ref[:NEAR, :] = cache_rows(cv_ref, cache_len - NEAR, NEAR, 1)
    kx_ref[NEAR:, :] = jnp.zeros((LANES, HEAD_DIM), BF16)
    vx_ref[NEAR:, :] = jnp.zeros((LANES, HEAD_DIM), BF16)
    kx_ref[NEAR:NEAR + 16, :] = jnp.concatenate([kn, jnp.zeros_like(kn)], axis=0).astype(BF16)
    vx_ref[NEAR:NEAR + 16, :] = jnp.concatenate([vn, jnp.zeros_like(vn)], axis=0).astype(BF16)

    def softmax_parts(s, valid, s_self=None):
        s = jnp.where(valid, s, NEG_INF)
        m = jnp.max(s, axis=-1, keepdims=True)
        if s_self is not None:
            m = jnp.maximum(m, s_self)
        p = jnp.exp(s - m)
        l = jnp.sum(p, axis=-1, keepdims=True)
        return p, m, l

    def trow(shape):
        return lax.broadcasted_iota(jnp.int32, shape, 0) & 3

    w1 = 2 * KEYS_BACK
    k1 = kx_ref[NEAR - KEYS_BACK:NEAR + KEYS_BACK, :]
    v1 = vx_ref[NEAR - KEYS_BACK:NEAR + KEYS_BACK, :]
    t = trow((Q_ROWS, w1))
    c = lax.broadcasted_iota(jnp.int32, (Q_ROWS, w1), 1)
    p, m1, l1 = softmax_parts(_dot_nt(qb, k1) * scale, (c >= t) & (c <= KEYS_BACK + t))
    o1 = _dot(p.astype(BF16), v1) / l1
    lse1 = m1 + jnp.log(l1)

    t = trow((Q_ROWS, ext))
    c = lax.broadcasted_iota(jnp.int32, (Q_ROWS, ext), 1)
    p, m4, l4 = softmax_parts(_dot_nt(qb, kx_ref[...]) * scale, ((c & 3) == t) & (c <= NEAR + t))
    o4 = _dot(p.astype(BF16), vx_ref[...]) / l4
    lse4 = m4 + jnp.log(l4)

    for tt in range(n_new):
        k16_ref[tt * KEYS_BACK:(tt + 1) * KEYS_BACK, :] = cache_rows(ck_ref, tt, KEYS_BACK, 16)
        v16_ref[tt * KEYS_BACK:(tt + 1) * KEYS_BACK, :] = cache_rows(cv_ref, tt, KEYS_BACK, 16)
    w16 = n_new * KEYS_BACK
    t = trow((Q_ROWS, w16))
    c = lax.broadcasted_iota(jnp.int32, (Q_ROWS, w16), 1)
    s_self = jnp.sum(qb.astype(F32) * kn.astype(BF16).astype(F32), axis=-1, keepdims=True) * scale
    p, m16, l16 = softmax_parts(_dot_nt(qb, k16_ref[...]) * scale, (c >> 7) == t, s_self)
    p_self = jnp.exp(s_self - m16)
    l16 = l16 + p_self
    o16 = (_dot(p.astype(BF16), v16_ref[...]) + p_self.astype(BF16).astype(F32) * vn.astype(BF16).astype(F32)) / l16
    lse16 = m16 + jnp.log(l16)

    m = jnp.maximum(jnp.maximum(lse1, lse4), lse16)
    e1, e4, e16 = jnp.exp(lse1 - m), jnp.exp(lse4 - m), jnp.exp(lse16 - m)
    return (e1 * o1 + e4 * o4 + e16 * o16) / (e1 + e4 + e16)


def _attn_sample_kernel(q_ref, kn_ref, vn_ref, ck_ref, cv_ref, o_ref, kx_ref, vx_ref, k16_ref, v16_ref,
                        *, cache_len, n_new):
    for h in range(A_HEADS):
        sl = slice(h * HEAD_DIM, (h + 1) * HEAD_DIM)
        o_ref[:, sl] = _attn_sample_head(h, q_ref[:, sl], kn_ref[:, sl], vn_ref[:, sl], ck_ref, cv_ref,
                                         kx_ref, vx_ref, k16_ref, v16_ref, cache_len=cache_len, n_new=n_new)


def _attn_sample(qn, kn, vn, cache_k, cache_v, n_new):
    bn, rows, _ = cache_k.shape
    cache_len = rows // A_HEADS
    assert KEYS_BACK == 128 and cache_len == A_BUF and n_new == 4, "written for a full window cache and 4 new tokens"
    new_spec = pl.BlockSpec((None, Q_ROWS, A_WIDTH), lambda b: (b, 0, 0))
    cache_spec = pl.BlockSpec((None, rows, HEAD_DIM), lambda b: (b, 0, 0))
    ext = NEAR + LANES
    return pl.pallas_call(
        functools.partial(_attn_sample_kernel, cache_len=cache_len, n_new=n_new),
        grid=(bn,),
        in_specs=[new_spec, new_spec, new_spec, cache_spec, cache_spec],
        out_specs=new_spec,
        out_shape=jax.ShapeDtypeStruct((bn, Q_ROWS, A_WIDTH), F32),
        scratch_shapes=[pltpu.VMEM((ext, HEAD_DIM), BF16)] * 2 + [pltpu.VMEM((n_new * KEYS_BACK, HEAD_DIM), BF16)] * 2,
        compiler_params=_params(("parallel",)),
        name="attn_sample",
    )(qn, kn, vn, cache_k, cache_v)


def _gla_kernel(q_ref, k_ref, lb_ref, v_ref, rb_ref, g_ref, s0_ref, o_ref, sfin_ref, st_ref, *, n_chunks, group):
    c = GLA_CHUNK
    st_ref[...] = s0_ref[...].T
    ti = lax.broadcasted_iota(jnp.int32, (c, c), 0)
    tj = lax.broadcasted_iota(jnp.int32, (c, c), 1)
    causal = tj <= ti

    def group_body(gi, carry):
        rows = [pl.ds(pl.multiple_of((gi * group + u) * c, c), c) for u in range(group)]
        q_dec, k_inv, k_end, decay = [], [], [], []
        for r in rows:
            b = lb_ref[r, :]
            b_last = b[c - 1:c, :]
            q = q_ref[r, :]
            k = k_ref[r, :]
            q_dec.append((q * jnp.exp(b)).astype(BF16))
            k_inv.append((k * jnp.exp(-b)).astype(BF16))
            k_end.append((k * jnp.exp(b_last - b)).astype(BF16))
            decay.append(jnp.exp(b_last))
        att = [jnp.where(causal, _dot_nt(qd, ki), 0.0).astype(BF16) for qd, ki in zip(q_dec, k_inv)]
        delta = [_dot_tn(v_ref[r, :], ke) for r, ke in zip(rows, k_end)]
        st = st_ref[...]
        before = []
        for dc, dl in zip(decay, delta):
            before.append(st.astype(BF16))
            st = st * dc + dl
        st_ref[...] = st
        for r, a, qd, sb in zip(rows, att, q_dec, before):
            o = _dot(a, v_ref[r, :]) + _dot_nt(qd, sb)
            gated = _rms_scale(o, g_ref[...]) * jax.nn.silu(rb_ref[r, :])
            o_ref[r, :] = gated.astype(o_ref.dtype)
        return carry

    lax.fori_loop(0, n_chunks // group, group_body, 0)
    sfin_ref[...] = st_ref[...].T


def _gla(qkb, lb, vb, rb, g, s0, out_dtype):
    bn, t, _ = qkb.shape
    assert t % GLA_CHUNK == 0
    n_chunks = t // GLA_CHUNK
    group = 8 if n_chunks % 8 == 0 else 1
    return pl.pallas_call(
        functools.partial(_gla_kernel, n_chunks=n_chunks, group=group),
        grid=(bn, B_HEADS),
        in_specs=[
            pl.BlockSpec((None, t, B_DK), lambda b, h: (b, 0, h)),
            pl.BlockSpec((None, t, B_DK), lambda b, h: (b, 0, B_HEADS + h)),
            pl.BlockSpec((None, t, B_DK), lambda b, h: (b, 0, h)),
            pl.BlockSpec((None, t, B_DV), lambda b, h: (b, 0, h)),
            pl.BlockSpec((None, t, B_DV), lambda b, h: (b, 0, h)),
            pl.BlockSpec((None, 1, B_DV), lambda b, h: (h, 0, 0)),
            pl.BlockSpec((None, None, B_DK, B_DV), lambda b, h: (b, h, 0, 0)),
        ],
        out_specs=(
            pl.BlockSpec((None, t, B_DV), lambda b, h: (b, 0, h)),
            pl.BlockSpec((None, None, B_DK, B_DV), lambda b, h: (b, h, 0, 0)),
        ),
        out_shape=(
            jax.ShapeDtypeStruct((bn, t, B_WIDTH), out_dtype),
            jax.ShapeDtypeStruct((bn, B_HEADS, B_DK, B_DV), F32),
        ),
        scratch_shapes=[pltpu.VMEM((B_DV, B_DK), F32)],
        compiler_params=_params(("parallel", "parallel")),
        name="gla",
    )(qkb, qkb, lb, vb, rb, g, s0)


def _outproj_kernel(x_ref, ma_ref, mb_ref, wa_ref, wb_ref, g_ref, x1_ref, hm_ref):
    x1 = x_ref[...] + _dot(ma_ref[...].astype(BF16), wa_ref[...]) + _dot(mb_ref[...].astype(BF16), wb_ref[...])
    x1_ref[...] = x1
    hm_ref[...] = _rms_scale(x1, g_ref[...]).astype(BF16)


def _outproj(x, mix_a, mix_b, w_a, w_b, g, tm):
    t, d = x.shape
    row = lambda i: (i, 0)
    const = lambda i: (0, 0)
    return pl.pallas_call(
        _outproj_kernel,
        grid=(t // tm,),
        in_specs=[
            pl.BlockSpec((tm, d), row),
            pl.BlockSpec((tm, A_WIDTH), row),
            pl.BlockSpec((tm, B_WIDTH), row),
            pl.BlockSpec((A_WIDTH, d), const),
            pl.BlockSpec((B_WIDTH, d), const),
            pl.BlockSpec((1, d), const),
        ],
        out_specs=(pl.BlockSpec((tm, d), row), pl.BlockSpec((tm, d), row)),
        out_shape=(jax.ShapeDtypeStruct((t, d), F32), jax.ShapeDtypeStruct((t, d), BF16)),
        compiler_params=_params(("parallel",)),
        name="outproj",
    )(x, mix_a, mix_b, w_a, w_b, g)


def _mlp_kernel(x1_ref, hm_ref, wu_ref, wd_ref, y_ref):
    f = pl.program_id(1)

    @pl.when(f == 0)
    def _():
        y_ref[...] = x1_ref[...]

    u = jnp.square(jnp.maximum(_dot(hm_ref[...], wu_ref[...]), 0.0))
    y_ref[...] += _dot(u.astype(BF16), wd_ref[...])


def _mlp(x1, hm, w_up, w_down, tm, tf):
    t, d = x1.shape
    d_ff = w_up.shape[1]
    row = lambda i, f: (i, 0)
    return pl.pallas_call(
        _mlp_kernel,
        grid=(t // tm, d_ff // tf),
        in_specs=[
            pl.BlockSpec((tm, d), row),
            pl.BlockSpec((tm, d), row),
            pl.BlockSpec((d, tf), lambda i, f: (0, f)),
            pl.BlockSpec((tf, d), lambda i, f: (f, 0)),
        ],
        out_specs=pl.BlockSpec((tm, d), row),
        out_shape=jax.ShapeDtypeStruct((t, d), F32),
        compiler_params=_params(("parallel", "arbitrary")),
        name="mlp",
    )(x1, hm, w_up, w_down)


def _row_tile(t):
    return 512 if t % 512 == 0 else t


def _layer_weights(attn_norm_g, w_in, q_norm_g, k_norm_g, w_gate2, b_gate, gla_norm_g, w_out, mlp_norm_g,
                   w_up, w_down):
    n_main = N_MAIN_TILES * A_WIDTH
    w_ga = jnp.pad(w_in[:, n_main:], ((0, 0), (0, LANES - GATE_RANK)))
    w_g2 = jnp.pad(w_gate2, ((0, LANES - GATE_RANK), (0, 0)))
    return dict(
        attn_g=attn_norm_g[None, :],
        w_main=w_in[:, :n_main].astype(BF16),
        w_ga=w_ga.astype(BF16),
        w_g2=w_g2.astype(BF16),
        b_gate=b_gate[None, :],
        q_g=q_norm_g[None, :],
        k_g=k_norm_g[None, :],
        gla_g=gla_norm_g[:, None, :],
        w_out_a=w_out[:A_WIDTH].astype(BF16),
        w_out_b=w_out[A_WIDTH:].astype(BF16),
        mlp_g=mlp_norm_g[None, :],
        w_up=w_up.astype(BF16),
        w_down=w_down.astype(BF16),
    )


def _mixer_inputs(x2d, w, chunk):
    t = x2d.shape[0]
    tm = 256 if t % 256 == 0 else t
    return _inproj(x2d, w["attn_g"], w["w_main"], w["w_ga"], w["w_g2"], w["b_gate"], w["q_g"], w["k_g"], tm, chunk)


def _finish(x2d, mix_a, mix_b, w):
    t = x2d.shape[0]
    x1, hm = _outproj(x2d, mix_a, mix_b, w["w_out_a"], w["w_out_b"], w["mlp_g"], _row_tile(t))
    if t % 1024 == 0:
        return _mlp(x1, hm, w["w_up"], w["w_down"], 1024, 512)
    return _mlp(x1, hm, w["w_up"], w["w_down"], _row_tile(t), 1024)


def _prompt_layer(xp, w):
    bn, seq, d = xp.shape
    x2d = xp.reshape(bn * seq, d)
    assert seq % GLA_CHUNK == 0, "GLA chunks must not straddle batch rows"
    qa, ka, va, kwin, vwin, qkb, vb, rb, lb = _mixer_inputs(x2d, w, GLA_CHUNK)
    b3 = lambda a: a.reshape(bn, seq, a.shape[-1])
    oa = _attn_prompt(b3(qa), b3(ka), b3(va))
    s0 = jnp.zeros((bn, B_HEADS, B_DK, B_DV), F32)
    ob, s_fin = _gla(b3(qkb), b3(lb), b3(vb), b3(rb), w["gla_g"], s0, BF16)
    y = _finish(x2d, oa.reshape(bn * seq, A_WIDTH), ob.reshape(bn * seq, B_WIDTH), w)
    lp = min(A_BUF, seq)
    win_k = kwin.reshape(bn, seq, A_HEADS, HEAD_DIM)[:, seq - lp:]
    win_v = vwin.reshape(bn, seq, A_HEADS, HEAD_DIM)[:, seq - lp:]
    return y.reshape(bn, seq, d), win_k, win_v, s_fin


def _sample_layer(xs, cache_k, cache_v, state, w):
    bn, n_new, d = xs.shape
    x2d = xs.reshape(bn * n_new, d)
    assert n_new <= GLA_CHUNK, "the new tokens of one batch row form a single GLA chunk"
    qa, ka, va, kwin, vwin, qkb, vb, rb, lb = _mixer_inputs(x2d, w, n_new)

    def padded(a, rows, mode="constant"):
        a = a.reshape(bn, n_new, a.shape[-1])
        return jnp.pad(a, ((0, 0), (0, rows - n_new), (0, 0)), mode=mode)

    cache_len = cache_k.shape[1]
    oa = _attn_sample(padded(qa, Q_ROWS), padded(ka, Q_ROWS), padded(va, Q_ROWS),
                      cache_k.reshape(bn, cache_len * A_HEADS, HEAD_DIM),
                      cache_v.reshape(bn, cache_len * A_HEADS, HEAD_DIM), n_new)
    c = GLA_CHUNK
    ob, s_new = _gla(padded(qkb, c), padded(lb, c, "edge"), padded(vb, c), padded(rb, c), w["gla_g"], state, F32)
    y = _finish(x2d, oa[:, :n_new].reshape(bn * n_new, A_WIDTH), ob[:, :n_new].reshape(bn * n_new, B_WIDTH), w)
    win_k = kwin.reshape(bn, n_new, A_HEADS, HEAD_DIM)
    win_v = vwin.reshape(bn, n_new, A_HEADS, HEAD_DIM)
    return y.reshape(bn, n_new, d), win_k, win_v, s_new


def kernel(x_prompt, x_sample, cache_win_k, cache_win_v, state_gla, attn_norm_g, w_in, q_norm_g, k_norm_g,
           w_gate2, b_gate, gla_norm_g, w_out, mlp_norm_g, w_up, w_down):
    depth = w_in.shape[0]
    xp, xs = x_prompt, x_sample
    pk, pv, ps, sk, sv, ss = [], [], [], [], [], []
    for l in range(depth):
        w = _layer_weights(attn_norm_g[l], w_in[l], q_norm_g[l], k_norm_g[l], w_gate2[l], b_gate[l],
                           gla_norm_g[l], w_out[l], mlp_norm_g[l], w_up[l], w_down[l])
        xp, k_l, v_l, s_l = _prompt_layer(xp, w)
        pk.append(k_l)
        pv.append(v_l)
        ps.append(s_l)
        xs, k_l, v_l, s_l = _sample_layer(xs, cache_win_k[l], cache_win_v[l], state_gla[l], w)
        sk.append(k_l)
        sv.append(v_l)
        ss.append(s_l)
    return (xp, xs, jnp.stack(pk), jnp.stack(pv), jnp.stack(ps), jnp.stack(sk), jnp.stack(sv), jnp.stack(ss))
```

```python
import functools

import jax
import jax.numpy as jnp
from jax import lax
from jax.experimental import pallas as pl
from jax.experimental.pallas import tpu as pltpu

F32 = jnp.float32
BF16 = jnp.bfloat16

HEAD_DIM = 128
A_HEADS = 8
A_WIDTH = A_HEADS * HEAD_DIM
DILATIONS = (1, 4, 16)
KEYS_BACK = 128
A_BUF = KEYS_BACK * max(DILATIONS)
B_HEADS = 4
B_DK = 128
B_DV = 256
B_WIDTH = B_HEADS * B_DV
GATE_RANK = 16
GATE_TAU = 16.0
GLA_CHUNK = 64
RMS_EPS = 1e-6
LANES = 128
NEG_INF = float("-inf")

VMEM_LIMIT_BYTES = 56 * 1024 * 1024


def _params(semantics):
    return pltpu.CompilerParams(dimension_semantics=semantics, vmem_limit_bytes=VMEM_LIMIT_BYTES)


def _dot(a, b):
    return jnp.dot(a, b, preferred_element_type=F32)


def _dot_nt(a, b):
    return lax.dot_general(a, b, (((1,), (1,)), ((), ())), preferred_element_type=F32)


def _dot_tn(a, b):
    return lax.dot_general(a, b, (((0,), (0,)), ((), ())), preferred_element_type=F32)


def _rms_scale(x, g):
    return x * lax.rsqrt(jnp.mean(x * x, axis=-1, keepdims=True) + RMS_EPS) * g


N_MAIN_TILES = 6


CUMSUM_ROWS = 128


def _inproj_kernel(x_ref, g_ref, w_ref, wg2_ref, bg_ref, qg_ref, kg_ref,
                   qa_ref, ka_ref, va_ref, kwin_ref, vwin_ref, qkb_ref, vb_ref, rb_ref, lb_ref, hn_ref,
                   *, chunk):
    tm = x_ref.shape[0]
    hn_ref[...] = _rms_scale(x_ref[...], g_ref[...]).astype(BF16)

    def proj(j):
        return _dot(hn_ref[...], w_ref[:, j * A_WIDTH:(j + 1) * A_WIDTH])

    def per_head(y, fn, out_ref, win_ref):
        for h in range(A_HEADS):
            sl = slice(h * HEAD_DIM, (h + 1) * HEAD_DIM)
            yh = fn(y[:, sl])
            out_ref[:, sl] = yh
            if win_ref is not None:
                win_ref[pl.ds(h, tm, stride=A_HEADS), :] = yh

    n_main = N_MAIN_TILES * A_WIDTH
    ga = _dot(hn_ref[...], w_ref[:, n_main:])
    z = _dot(ga.astype(BF16), wg2_ref[...]) + bg_ref[...]
    log_a = (jnp.minimum(z, 0.0) - jnp.log1p(jnp.exp(-jnp.abs(z)))) * (1.0 / GATE_TAU)

    per_head(proj(0), lambda y: _rms_scale(y, qg_ref[...]), qa_ref, None)

    shift = chunk.bit_length() - 1
    ti = lax.broadcasted_iota(jnp.int32, (CUMSUM_ROWS, CUMSUM_ROWS), 0)
    tj = lax.broadcasted_iota(jnp.int32, (CUMSUM_ROWS, CUMSUM_ROWS), 1)
    tri = (((ti >> shift) == (tj >> shift)) & (tj <= ti)).astype(F32)
    for r in range(tm // CUMSUM_ROWS):
        rows = slice(r * CUMSUM_ROWS, (r + 1) * CUMSUM_ROWS)
        lb_ref[rows, :] = jnp.dot(tri, log_a[rows, :], preferred_element_type=F32,
                                  precision=lax.Precision.HIGHEST)

    per_head(proj(1), lambda y: _rms_scale(y, kg_ref[...]), ka_ref, kwin_ref)
    per_head(proj(2), lambda y: y, va_ref, vwin_ref)

    y = proj(3)
    half = B_HEADS * B_DK
    qkb_ref[:, :half] = y[:, :half] * (B_DK ** -0.5)
    qkb_ref[:, half:] = y[:, half:]

    vb_ref[...] = proj(4).astype(BF16)
    rb_ref[...] = proj(5)


def _inproj(x, g, w_all, w_g2, b_gate, q_g, k_g, tm, chunk):
    t, d = x.shape
    assert chunk & (chunk - 1) == 0 and CUMSUM_ROWS % chunk == 0 and tm % CUMSUM_ROWS == 0
    assert w_all.shape == (d, N_MAIN_TILES * A_WIDTH + LANES)
    row = lambda i: (i, 0)
    const = lambda i: (0, 0)
    resident = dict(pipeline_mode=pl.Buffered(1))
    out_shapes = (
        jax.ShapeDtypeStruct((t, A_WIDTH), F32),
        jax.ShapeDtypeStruct((t, A_WIDTH), F32),
        jax.ShapeDtypeStruct((t, A_WIDTH), F32),
        jax.ShapeDtypeStruct((t * A_HEADS, HEAD_DIM), F32),
        jax.ShapeDtypeStruct((t * A_HEADS, HEAD_DIM), F32),
        jax.ShapeDtypeStruct((t, 2 * B_HEADS * B_DK), F32),
        jax.ShapeDtypeStruct((t, B_WIDTH), BF16),
        jax.ShapeDtypeStruct((t, B_WIDTH), F32),
        jax.ShapeDtypeStruct((t, B_HEADS * B_DK), F32),
    )
    return pl.pallas_call(
        functools.partial(_inproj_kernel, chunk=chunk),
        grid=(t // tm,),
        in_specs=[
            pl.BlockSpec((tm, d), row),
            pl.BlockSpec((1, d), const),
            pl.BlockSpec(w_all.shape, const, **resident),
            pl.BlockSpec((LANES, B_HEADS * B_DK), const),
            pl.BlockSpec((1, B_HEADS * B_DK), const),
            pl.BlockSpec((1, HEAD_DIM), const),
            pl.BlockSpec((1, HEAD_DIM), const),
        ],
        out_specs=tuple(pl.BlockSpec((tm * s.shape[0] // t, s.shape[1]), row) for s in out_shapes),
        out_shape=out_shapes,
        scratch_shapes=[pltpu.VMEM((tm, d), BF16)],
        compiler_params=_params(("parallel",)),
        name="inproj",
    )(x, g, w_all, w_g2, b_gate, q_g, k_g)


BLK = KEYS_BACK


def _attend_blocks(blocks):
    scores = []
    for q_ref, k_ref, _, start, back in blocks:
        q = q_ref[pl.ds(start, BLK), :]
        k = k_ref[pl.ds(start - back, BLK + back), :]
        s = _dot_nt(q, k) * (HEAD_DIM ** -0.5)
        qi = lax.broadcasted_iota(jnp.int32, s.shape, 0)
        kj = lax.broadcasted_iota(jnp.int32, s.shape, 1)
        valid = (kj >= qi) & (kj <= qi + BLK) if back else kj <= qi
        scores.append(jnp.where(valid, s, NEG_INF))
    probs = []
    for s in scores:
        m = jnp.max(s, axis=-1, keepdims=True)
        p = jnp.exp(s - m)
        probs.append((p.astype(BF16), m, jnp.sum(p, axis=-1, keepdims=True)))
    outs = []
    for (p, m, l), (_, _, v_ref, start, back) in zip(probs, blocks):
        v = v_ref[pl.ds(start - back, BLK + back), :]
        outs.append((_dot(p, v) / l, m + jnp.log(l)))
    return outs


def _attn_prompt_kernel(q_ref, k_ref, v_ref, o_ref,
                        q1_ref, k1_ref, v1_ref, q4_ref, k4_ref, v4_ref, q16_ref, k16_ref, v16_ref,
                        q4f_ref, k4f_ref, v4f_ref,
                        o1_ref, l1_ref, o4_ref, l4_ref, o16_ref, l16_ref,
                        *, seq):
    def store(o_out, l_out, pos, stride, o, lse):
        if stride == 1:
            idx = pl.ds(pl.multiple_of(pos, BLK), BLK)
        else:
            idx = pl.ds(pos, BLK, stride=stride)
        o_out[idx, :] = o
        l_out[idx, :] = jnp.broadcast_to(lse, (BLK, LANES))

    sub4, sub16 = seq // 4, seq // 16
    for x_ref, x1_ref, x4_ref, x4f_ref, x16_ref in ((q_ref, q1_ref, q4_ref, q4f_ref, q16_ref),
                                                   (k_ref, k1_ref, k4_ref, k4f_ref, k16_ref),
                                                   (v_ref, v1_ref, v4_ref, v4f_ref, v16_ref)):
        x1_ref[...] = x_ref[...].astype(BF16)
        for r in range(4):
            x4 = x_ref[pl.ds(r, sub4, stride=4), :]
            x4f_ref[r * sub4:(r + 1) * sub4, :] = x4
            x4_ref[r * sub4:(r + 1) * sub4, :] = x4.astype(BF16)
        for r in range(16):
            x16 = x4f_ref[pl.ds((r % 4) * sub4 + r // 4, sub16, stride=4), :]
            x16_ref[r * sub16:(r + 1) * sub16, :] = x16.astype(BF16)

    d1 = (q1_ref, k1_ref, v1_ref)
    d4 = (q4_ref, k4_ref, v4_ref)
    d16 = (q16_ref, k16_ref, v16_ref)

    def run(blocks, dests):
        for (o, lse), (o_out, l_out, pos, stride) in zip(_attend_blocks(blocks), dests):
            store(o_out, l_out, pos, stride, o, lse)

    run([(*d1, 0, 0)] + [(*d4, r * sub4, 0) for r in range(4)],
        [(o1_ref, l1_ref, 0, 1)] + [(o4_ref, l4_ref, r, 4) for r in range(4)])

    group16 = 16

    def d16_body(i, carry):
        rs = [i * group16 + u for u in range(group16)]
        run([(*d16, pl.multiple_of(r * sub16, BLK), 0) for r in rs],
            [(o16_ref, l16_ref, r, 16) for r in rs])
        return carry

    lax.fori_loop(0, 16 // group16, d16_body, 0)

    group1 = 15

    def d1_body(i, carry):
        ns = [1 + i * group1 + u for u in range(group1)]
        run([(*d1, pl.multiple_of(n * BLK, BLK), BLK) for n in ns],
            [(o1_ref, l1_ref, n * BLK, 1) for n in ns])
        return carry

    lax.fori_loop(0, (seq // BLK - 1) // group1, d1_body, 0)

    rn = [(r, n) for n in range(1, sub4 // BLK) for r in range(4)]
    run([(*d4, r * sub4 + n * BLK, BLK) for r, n in rn],
        [(o4_ref, l4_ref, r + n * BLK * 4, 4) for r, n in rn])

    rows = 256

    def merge_body(c, carry):
        sl = pl.ds(pl.multiple_of(c * rows, rows), rows)
        l1, l4, l16 = l1_ref[sl, :], l4_ref[sl, :], l16_ref[sl, :]
        m = jnp.maximum(jnp.maximum(l1, l4), l16)
        e1, e4, e16 = jnp.exp(l1 - m), jnp.exp(l4 - m), jnp.exp(l16 - m)
        num = e1 * o1_ref[sl, :] + e4 * o4_ref[sl, :] + e16 * o16_ref[sl, :]
        o_ref[sl, :] = (num / (e1 + e4 + e16)).astype(o_ref.dtype)
        return carry

    lax.fori_loop(0, seq // rows, merge_body, 0)


def _attn_prompt(qa, ka, va):
    bn, seq, _ = qa.shape
    assert seq == A_BUF, "prompt attention is written for a sequence of exactly the largest window"
    spec = pl.BlockSpec((None, seq, HEAD_DIM), lambda b, h: (b, 0, h))
    return pl.pallas_call(
        functools.partial(_attn_prompt_kernel, seq=seq),
        grid=(bn, A_HEADS),
        in_specs=[spec, spec, spec],
        out_specs=spec,
        out_shape=jax.ShapeDtypeStruct((bn, seq, A_WIDTH), BF16),
        scratch_shapes=[pltpu.VMEM((seq, HEAD_DIM), BF16)] * 9 + [pltpu.VMEM((seq, LANES), F32)] * 9,
        compiler_params=_params(("parallel", "parallel")),
        name="attn_prompt",
    )(qa, ka, va)


Q_ROWS = 8
NEAR = 4 * KEYS_BACK


def _attn_sample_kernel(q_ref, kn_ref, vn_ref, nk_ref, nv_ref, fk_ref, fv_ref, o_ref, *, n_new):
    scale = HEAD_DIM ** -0.5
    ext = NEAR + LANES
    total = ext + n_new * KEYS_BACK

    t = lax.broadcasted_iota(jnp.int32, (Q_ROWS, total), 0) & 3
    c = lax.broadcasted_iota(jnp.int32, (Q_ROWS, total), 1)
    c1 = c - (NEAR - KEYS_BACK)
    in1 = (c1 >= t) & (c1 <= KEYS_BACK + t)
    in4 = ((c & 3) == t) & (c <= NEAR + t)
    in16 = ((c - ext) >> 7) == t
    mult = in1.astype(F32) + in4.astype(F32) + in16.astype(F32)

    def rows_of(near_ref, far_ref, new, h):
        near = near_ref[pl.ds(h, NEAR, stride=A_HEADS), :]
        far = [far_ref[:, tt * A_HEADS + h, :] for tt in range(n_new)]
        pad = jnp.zeros((LANES - Q_ROWS, HEAD_DIM), F32)
        return jnp.concatenate([near, new, pad] + far, axis=0).astype(BF16)

    heads = range(A_HEADS)
    sl = lambda h: slice(h * HEAD_DIM, (h + 1) * HEAD_DIM)
    qs = [q_ref[:, sl(h)].astype(BF16) for h in heads]
    kns = [kn_ref[:, sl(h)] for h in heads]
    vns = [vn_ref[:, sl(h)] for h in heads]
    scores = [_dot_nt(qs[h], rows_of(nk_ref, fk_ref, kns[h], h)) * scale for h in heads]
    probs = []
    for h in heads:
        s_self = jnp.sum(qs[h].astype(F32) * kns[h].astype(BF16).astype(F32), axis=-1, keepdims=True) * scale
        s = jnp.where(mult > 0.0, scores[h], NEG_INF)
        m = jnp.maximum(jnp.max(s, axis=-1, keepdims=True), s_self)
        p = jnp.exp(s - m) * mult
        p_self = jnp.exp(s_self - m)
        probs.append((p.astype(BF16), p_self, jnp.sum(p, axis=-1, keepdims=True) + p_self))
    for h in heads:
        p, p_self, l = probs[h]
        acc = _dot(p, rows_of(nv_ref, fv_ref, vns[h], h))
        acc = acc + p_self.astype(BF16).astype(F32) * vns[h].astype(BF16).astype(F32)
        o_ref[:, sl(h)] = acc / l


def _attn_sample(qn, kn, vn, cache_k, cache_v, n_new):
    bn, cache_len = cache_k.shape[:2]
    assert KEYS_BACK == 128 and cache_len == A_BUF and n_new == 4, "written for a full window cache and 4 new tokens"
    new_spec = pl.BlockSpec((None, Q_ROWS, A_WIDTH), lambda b: (b, 0, 0))
    near_spec = pl.BlockSpec((None, NEAR * A_HEADS, HEAD_DIM), lambda b: (b, cache_len // NEAR - 1, 0))
    far_spec = pl.BlockSpec((None, KEYS_BACK, n_new * A_HEADS, HEAD_DIM), lambda b: (b, 0, 0, 0))
    near = lambda c: c.reshape(bn, cache_len * A_HEADS, HEAD_DIM)
    far = lambda c: c.reshape(bn, KEYS_BACK, 16 * A_HEADS, HEAD_DIM)
    return pl.pallas_call(
        functools.partial(_attn_sample_kernel, n_new=n_new),
        grid=(bn,),
        in_specs=[new_spec, new_spec, new_spec, near_spec, near_spec, far_spec, far_spec],
        out_specs=new_spec,
        out_shape=jax.ShapeDtypeStruct((bn, Q_ROWS, A_WIDTH), F32),
        compiler_params=_params(("parallel",)),
        name="attn_sample",
    )(qn, kn, vn, near(cache_k), near(cache_v), far(cache_k), far(cache_v))


def _gla_kernel(q_ref, k_ref, lb_ref, v_ref, rb_ref, g_ref, s0_ref, o_ref, sfin_ref, st_ref,
                *, n_chunks, group, heads):
    c = GLA_CHUNK
    for hd in range(heads):
        st_ref[hd] = s0_ref[hd].T
    ti = lax.broadcasted_iota(jnp.int32, (c, c), 0)
    tj = lax.broadcasted_iota(jnp.int32, (c, c), 1)
    causal = tj <= ti

    def group_body(gi, carry):
        rows = [pl.ds(pl.multiple_of((gi * group + u) * c, c), c) for u in range(group)]
        items = [(hd, r) for hd in range(heads) for r in rows]
        dk = lambda hd: slice(hd * B_DK, (hd + 1) * B_DK)
        dv = lambda hd: slice(hd * B_DV, (hd + 1) * B_DV)
        q_dec, k_inv, k_end, decay = [], [], [], []
        for hd, r in items:
            b = lb_ref[r, dk(hd)]
            b_last = b[c - 1:c, :]
            q = q_ref[r, dk(hd)]
            k = k_ref[r, dk(hd)]
            q_dec.append((q * jnp.exp(b)).astype(BF16))
            k_inv.append((k * jnp.exp(-b)).astype(BF16))
            k_end.append((k * jnp.exp(b_last - b)).astype(BF16))
            decay.append(jnp.exp(b_last))
        att = [jnp.where(causal, _dot_nt(qd, ki), 0.0).astype(BF16) for qd, ki in zip(q_dec, k_inv)]
        delta = [_dot_tn(v_ref[r, dv(hd)], ke) for (hd, r), ke in zip(items, k_end)]
        before = []
        for hd in range(heads):
            st = st_ref[hd]
            for i in range(hd * group, (hd + 1) * group):
                before.append(st.astype(BF16))
                st = st * decay[i] + delta[i]
            st_ref[hd] = st
        for (hd, r), a, qd, sb in zip(items, att, q_dec, before):
            o = _dot(a, v_ref[r, dv(hd)]) + _dot_nt(qd, sb)
            gated = _rms_scale(o, g_ref[hd]) * jax.nn.silu(rb_ref[r, dv(hd)])
            o_ref[r, dv(hd)] = gated.astype(o_ref.dtype)
        return carry

    lax.fori_loop(0, n_chunks // group, group_body, 0)
    for hd in range(heads):
        sfin_ref[hd] = st_ref[hd].T


def _gla(qkb, lb, vb, rb, g, s0, out_dtype):
    bn, t, _ = qkb.shape
    assert t % GLA_CHUNK == 0
    n_chunks = t // GLA_CHUNK
    heads, group = (1, 16) if n_chunks % 16 == 0 else (B_HEADS, 1)
    nhb = B_HEADS // heads
    return pl.pallas_call(
        functools.partial(_gla_kernel, n_chunks=n_chunks, group=group, heads=heads),
        grid=(bn, nhb),
        in_specs=[
            pl.BlockSpec((None, t, heads * B_DK), lambda b, h: (b, 0, h)),
            pl.BlockSpec((None, t, heads * B_DK), lambda b, h: (b, 0, nhb + h)),
            pl.BlockSpec((None, t, heads * B_DK), lambda b, h: (b, 0, h)),
            pl.BlockSpec((None, t, heads * B_DV), lambda b, h: (b, 0, h)),
            pl.BlockSpec((None, t, heads * B_DV), lambda b, h: (b, 0, h)),
            pl.BlockSpec((heads, 1, B_DV), lambda b, h: (h, 0, 0)),
            pl.BlockSpec((None, heads, B_DK, B_DV), lambda b, h: (b, h, 0, 0)),
        ],
        out_specs=(
            pl.BlockSpec((None, t, heads * B_DV), lambda b, h: (b, 0, h)),
            pl.BlockSpec((None, heads, B_DK, B_DV), lambda b, h: (b, h, 0, 0)),
        ),
        out_shape=(
            jax.ShapeDtypeStruct((bn, t, B_WIDTH), out_dtype),
            jax.ShapeDtypeStruct((bn, B_HEADS, B_DK, B_DV), F32),
        ),
        scratch_shapes=[pltpu.VMEM((heads, B_DV, B_DK), F32)],
        compiler_params=_params(("parallel", "parallel")),
        name="gla",
    )(qkb, qkb, lb, vb, rb, g, s0)


def _outproj_kernel(x_ref, ma_ref, mb_ref, wa_ref, wb_ref, g_ref, x1_ref, hm_ref):
    x1 = x_ref[...] + _dot(ma_ref[...].astype(BF16), wa_ref[...]) + _dot(mb_ref[...].astype(BF16), wb_ref[...])
    x1_ref[...] = x1
    hm_ref[...] = _rms_scale(x1, g_ref[...]).astype(BF16)


def _outproj(x, mix_a, mix_b, w_a, w_b, g, tm):
    t, d = x.shape
    row = lambda i: (i, 0)
    const = lambda i: (0, 0)
    return pl.pallas_call(
        _outproj_kernel,
        grid=(t // tm,),
        in_specs=[
            pl.BlockSpec((tm, d), row),
            pl.BlockSpec((tm, A_WIDTH), row),
            pl.BlockSpec((tm, B_WIDTH), row),
            pl.BlockSpec((A_WIDTH, d), const),
            pl.BlockSpec((B_WIDTH, d), const),
            pl.BlockSpec((1, d), const),
        ],
        out_specs=(pl.BlockSpec((tm, d), row), pl.BlockSpec((tm, d), row)),
        out_shape=(jax.ShapeDtypeStruct((t, d), F32), jax.ShapeDtypeStruct((t, d), BF16)),
        compiler_params=_params(("parallel",)),
        name="outproj",
    )(x, mix_a, mix_b, w_a, w_b, g)


def _mlp_kernel(x1_ref, hm_ref, wu_ref, wd_ref, y_ref):
    f = pl.program_id(1)

    @pl.when(f == 0)
    def _():
        y_ref[...] = x1_ref[...]

    u = jnp.square(jnp.maximum(_dot(hm_ref[...], wu_ref[...]), 0.0))
    y_ref[...] += _dot(u.astype(BF16), wd_ref[...])


def _mlp(x1, hm, w_up, w_down, tm, tf):
    t, d = x1.shape
    d_ff = w_up.shape[1]
    row = lambda i, f: (i, 0)
    return pl.pallas_call(
        _mlp_kernel,
        grid=(t // tm, d_ff // tf),
        in_specs=[
            pl.BlockSpec((tm, d), row),
            pl.BlockSpec((tm, d), row),
            pl.BlockSpec((d, tf), lambda i, f: (0, f)),
            pl.BlockSpec((tf, d), lambda i, f: (f, 0)),
        ],
        out_specs=pl.BlockSpec((tm, d), row),
        out_shape=jax.ShapeDtypeStruct((t, d), F32),
        compiler_params=_params(("parallel", "arbitrary")),
        name="mlp",
    )(x1, hm, w_up, w_down)


def _row_tile(t):
    return 512 if t % 512 == 0 else t


def _layer_weights(attn_norm_g, w_in, q_norm_g, k_norm_g, w_gate2, b_gate, gla_norm_g, w_out, mlp_norm_g,
                   w_up, w_down):
    w_g2 = jnp.pad(w_gate2, ((0, LANES - GATE_RANK), (0, 0)))
    return dict(
        attn_g=attn_norm_g[None, :],
        w_all=jnp.pad(w_in.astype(BF16), ((0, 0), (0, LANES - GATE_RANK))),
        w_g2=w_g2.astype(BF16),
        b_gate=b_gate[None, :],
        q_g=q_norm_g[None, :],
        k_g=k_norm_g[None, :],
        gla_g=gla_norm_g[:, None, :],
        w_out_a=w_out[:A_WIDTH].astype(BF16),
        w_out_b=w_out[A_WIDTH:].astype(BF16),
        mlp_g=mlp_norm_g[None, :],
        w_up=w_up.astype(BF16),
        w_down=w_down.astype(BF16),
    )


def _mixer_inputs(x2d, w, chunk):
    t = x2d.shape[0]
    tm = 256 if t % 256 == 0 else t
    return _inproj(x2d, w["attn_g"], w["w_all"], w["w_g2"], w["b_gate"], w["q_g"], w["k_g"], tm, chunk)


def _finish(x2d, mix_a, mix_b, w):
    t = x2d.shape[0]
    x1, hm = _outproj(x2d, mix_a, mix_b, w["w_out_a"], w["w_out_b"], w["mlp_g"], _row_tile(t))
    return _mlp(x1, hm, w["w_up"], w["w_down"], _row_tile(t), 1024)


def _prompt_layer(xp, w):
    bn, seq, d = xp.shape
    x2d = xp.reshape(bn * seq, d)
    assert seq % GLA_CHUNK == 0, "GLA chunks must not straddle batch rows"
    qa, ka, va, kwin, vwin, qkb, vb, rb, lb = _mixer_inputs(x2d, w, GLA_CHUNK)
    b3 = lambda a: a.reshape(bn, seq, a.shape[-1])
    oa = _attn_prompt(b3(qa), b3(ka), b3(va))
    s0 = jnp.zeros((bn, B_HEADS, B_DK, B_DV), F32)
    ob, s_fin = _gla(b3(qkb), b3(lb), b3(vb), b3(rb), w["gla_g"], s0, BF16)
    y = _finish(x2d, oa.reshape(bn * seq, A_WIDTH), ob.reshape(bn * seq, B_WIDTH), w)
    lp = min(A_BUF, seq)
    win_k = kwin.reshape(bn, seq, A_HEADS, HEAD_DIM)[:, seq - lp:]
    win_v = vwin.reshape(bn, seq, A_HEADS, HEAD_DIM)[:, seq - lp:]
    return y.reshape(bn, seq, d), win_k, win_v, s_fin


def _sample_layer(xs, cache_k, cache_v, state, w):
    bn, n_new, d = xs.shape
    x2d = xs.reshape(bn * n_new, d)
    assert n_new <= GLA_CHUNK, "the new tokens of one batch row form a single GLA chunk"
    qa, ka, va, kwin, vwin, qkb, vb, rb, lb = _mixer_inputs(x2d, w, n_new)

    def padded(a, rows, mode="constant"):
        a = a.reshape(bn, n_new, a.shape[-1])
        return jnp.pad(a, ((0, 0), (0, rows - n_new), (0, 0)), mode=mode)

    oa = _attn_sample(padded(qa, Q_ROWS), padded(ka, Q_ROWS), padded(va, Q_ROWS), cache_k, cache_v, n_new)
    c = GLA_CHUNK
    ob, s_new = _gla(padded(qkb, c), padded(lb, c, "edge"), padded(vb, c), padded(rb, c), w["gla_g"], state, F32)
    y = _finish(x2d, oa[:, :n_new].reshape(bn * n_new, A_WIDTH), ob[:, :n_new].reshape(bn * n_new, B_WIDTH), w)
    win_k = kwin.reshape(bn, n_new, A_HEADS, HEAD_DIM)
    win_v = vwin.reshape(bn, n_new, A_HEADS, HEAD_DIM)
    return y.reshape(bn, n_new, d), win_k, win_v, s_new


def kernel(x_prompt, x_sample, cache_win_k, cache_win_v, state_gla, attn_norm_g, w_in, q_norm_g, k_norm_g,
           w_gate2, b_gate, gla_norm_g, w_out, mlp_norm_g, w_up, w_down):
    depth = w_in.shape[0]
    xp, xs = x_prompt, x_sample
    pk, pv, ps, sk, sv, ss = [], [], [], [], [], []
    for l in range(depth):
        w = _layer_weights(attn_norm_g[l], w_in[l], q_norm_g[l], k_norm_g[l], w_gate2[l], b_gate[l],
                           gla_norm_g[l], w_out[l], mlp_norm_g[l], w_up[l], w_down[l])
        xp, k_l, v_l, s_l = _prompt_layer(xp, w)
        pk.append(k_l)
        pv.append(v_l)
        ps.append(s_l)
        xs, k_l, v_l, s_l = _sample_layer(xs, cache_win_k[l], cache_win_v[l], state_gla[l], w)
        sk.append(k_l)
        sv.append(v_l)
        ss.append(s_l)
    return (xp, xs, jnp.stack(pk), jnp.stack(pv), jnp.stack(ps), jnp.stack(sk), jnp.stack(sv), jnp.stack(ss))
```

```python
import functools

import jax
import jax.numpy as jnp
from jax import lax
from jax.experimental import pallas as pl
from jax.experimental.pallas import tpu as pltpu

F32 = jnp.float32
BF16 = jnp.bfloat16

HEAD_DIM = 128
A_HEADS = 8
A_WIDTH = A_HEADS * HEAD_DIM
DILATIONS = (1, 4, 16)
KEYS_BACK = 128
A_BUF = KEYS_BACK * max(DILATIONS)
B_HEADS = 4
B_DK = 128
B_DV = 256
B_WIDTH = B_HEADS * B_DV
GATE_RANK = 16
GATE_TAU = 16.0
GLA_CHUNK = 64
RMS_EPS = 1e-6
LANES = 128
NEG_INF = float("-inf")

VMEM_LIMIT_BYTES = 56 * 1024 * 1024


def _params(semantics):
    return pltpu.CompilerParams(dimension_semantics=semantics, vmem_limit_bytes=VMEM_LIMIT_BYTES)


def _dot(a, b):
    return jnp.dot(a, b, preferred_element_type=F32)


def _dot_nt(a, b):
    return lax.dot_general(a, b, (((1,), (1,)), ((), ())), preferred_element_type=F32)


def _dot_tn(a, b):
    return lax.dot_general(a, b, (((0,), (0,)), ((), ())), preferred_element_type=F32)


def _rms_scale(x, g):
    return x * lax.rsqrt(jnp.mean(x * x, axis=-1, keepdims=True) + RMS_EPS) * g


N_MAIN_TILES = 6


CUMSUM_ROWS = 128


def _inproj_kernel(x_ref, g_ref, w_ref, wg2_ref, bg_ref, qg_ref, kg_ref,
                   qa_ref, ka_ref, va_ref, kwin_ref, vwin_ref, qkb_ref, vb_ref, rb_ref, lb_ref, hn_ref,
                   *, chunk):
    tm = x_ref.shape[0]
    hn_ref[...] = _rms_scale(x_ref[...], g_ref[...]).astype(BF16)

    def proj(j):
        return _dot(hn_ref[...], w_ref[:, j * A_WIDTH:(j + 1) * A_WIDTH])

    def per_head(y, fn, out_ref, win_ref):
        for h in range(A_HEADS):
            sl = slice(h * HEAD_DIM, (h + 1) * HEAD_DIM)
            yh = fn(y[:, sl])
            out_ref[:, sl] = yh
            if win_ref is not None:
                win_ref[pl.ds(h, tm, stride=A_HEADS), :] = yh

    n_main = N_MAIN_TILES * A_WIDTH
    ga = _dot(hn_ref[...], w_ref[:, n_main:])
    z = _dot(ga.astype(BF16), wg2_ref[...]) + bg_ref[...]
    log_a = (jnp.minimum(z, 0.0) - jnp.log1p(jnp.exp(-jnp.abs(z)))) * (1.0 / GATE_TAU)

    per_head(proj(0), lambda y: _rms_scale(y, qg_ref[...]), qa_ref, None)

    shift = chunk.bit_length() - 1
    ti = lax.broadcasted_iota(jnp.int32, (CUMSUM_ROWS, CUMSUM_ROWS), 0)
    tj = lax.broadcasted_iota(jnp.int32, (CUMSUM_ROWS, CUMSUM_ROWS), 1)
    tri = (((ti >> shift) == (tj >> shift)) & (tj <= ti)).astype(F32)
    for r in range(tm // CUMSUM_ROWS):
        rows = slice(r * CUMSUM_ROWS, (r + 1) * CUMSUM_ROWS)
        lb_ref[rows, :] = jnp.dot(tri, log_a[rows, :], preferred_element_type=F32,
                                  precision=lax.Precision.HIGHEST)

    per_head(proj(1), lambda y: _rms_scale(y, kg_ref[...]), ka_ref, kwin_ref)
    per_head(proj(2), lambda y: y, va_ref, vwin_ref)

    y = proj(3)
    half = B_HEADS * B_DK
    qkb_ref[:, :half] = y[:, :half] * (B_DK ** -0.5)
    qkb_ref[:, half:] = y[:, half:]

    vb_ref[...] = proj(4).astype(BF16)
    rb_ref[...] = proj(5)


def _inproj(x, g, w_all, w_g2, b_gate, q_g, k_g, tm, chunk):
    t, d = x.shape
    assert chunk & (chunk - 1) == 0 and CUMSUM_ROWS % chunk == 0 and tm % CUMSUM_ROWS == 0
    assert w_all.shape == (d, N_MAIN_TILES * A_WIDTH + LANES)
    row = lambda i: (i, 0)
    const = lambda i: (0, 0)
    resident = dict(pipeline_mode=pl.Buffered(1))
    out_shapes = (
        jax.ShapeDtypeStruct((t, A_WIDTH), F32),
        jax.ShapeDtypeStruct((t, A_WIDTH), F32),
        jax.ShapeDtypeStruct((t, A_WIDTH), F32),
        jax.ShapeDtypeStruct((t * A_HEADS, HEAD_DIM), F32),
        jax.ShapeDtypeStruct((t * A_HEADS, HEAD_DIM), F32),
        jax.ShapeDtypeStruct((t, 2 * B_HEADS * B_DK), F32),
        jax.ShapeDtypeStruct((t, B_WIDTH), BF16),
        jax.ShapeDtypeStruct((t, B_WIDTH), F32),
        jax.ShapeDtypeStruct((t, B_HEADS * B_DK), F32),
    )
    return pl.pallas_call(
        functools.partial(_inproj_kernel, chunk=chunk),
        grid=(t // tm,),
        in_specs=[
            pl.BlockSpec((tm, d), row),
            pl.BlockSpec((1, d), const),
            pl.BlockSpec(w_all.shape, const, **resident),
            pl.BlockSpec((LANES, B_HEADS * B_DK), const),
            pl.BlockSpec((1, B_HEADS * B_DK), const),
            pl.BlockSpec((1, HEAD_DIM), const),
            pl.BlockSpec((1, HEAD_DIM), const),
        ],
        out_specs=tuple(pl.BlockSpec((tm * s.shape[0] // t, s.shape[1]), row) for s in out_shapes),
        out_shape=out_shapes,
        scratch_shapes=[pltpu.VMEM((tm, d), BF16)],
        compiler_params=_params(("parallel",)),
        name="inproj",
    )(x, g, w_all, w_g2, b_gate, q_g, k_g)


BLK = KEYS_BACK


def _attend_blocks(blocks):
    scores = []
    for q_ref, k_ref, _, start, back in blocks:
        q = q_ref[pl.ds(start, BLK), :]
        k = k_ref[pl.ds(start - back, BLK + back), :]
        s = _dot_nt(q, k) * (HEAD_DIM ** -0.5)
        qi = lax.broadcasted_iota(jnp.int32, s.shape, 0)
        kj = lax.broadcasted_iota(jnp.int32, s.shape, 1)
        valid = (kj >= qi) & (kj <= qi + BLK) if back else kj <= qi
        scores.append(jnp.where(valid, s, NEG_INF))
    probs = []
    for s in scores:
        m = jnp.max(s, axis=-1, keepdims=True)
        p = jnp.exp(s - m)
        probs.append((p.astype(BF16), m, jnp.sum(p, axis=-1, keepdims=True)))
    outs = []
    for (p, m, l), (_, _, v_ref, start, back) in zip(probs, blocks):
        v = v_ref[pl.ds(start - back, BLK + back), :]
        outs.append((_dot(p, v) / l, m + jnp.log(l)))
    return outs


def _attn_prompt_kernel(q_ref, k_ref, v_ref, o_ref,
                        q1_ref, k1_ref, v1_ref, q4_ref, k4_ref, v4_ref, q16_ref, k16_ref, v16_ref,
                        q4f_ref, k4f_ref, v4f_ref,
                        o1_ref, l1_ref, o4_ref, l4_ref, o16_ref, l16_ref,
                        *, seq):
    def store(o_out, l_out, pos, stride, o, lse):
        if stride == 1:
            idx = pl.ds(pl.multiple_of(pos, BLK), BLK)
        else:
            idx = pl.ds(pos, BLK, stride=stride)
        o_out[idx, :] = o
        l_out[idx, :] = jnp.broadcast_to(lse, (BLK, LANES))

    sub4, sub16 = seq // 4, seq // 16
    for x_ref, x1_ref, x4_ref, x4f_ref, x16_ref in ((q_ref, q1_ref, q4_ref, q4f_ref, q16_ref),
                                                   (k_ref, k1_ref, k4_ref, k4f_ref, k16_ref),
                                                   (v_ref, v1_ref, v4_ref, v4f_ref, v16_ref)):
        x1_ref[...] = x_ref[...].astype(BF16)
        for r in range(4):
            x4 = x_ref[pl.ds(r, sub4, stride=4), :]
            x4f_ref[r * sub4:(r + 1) * sub4, :] = x4
            x4_ref[r * sub4:(r + 1) * sub4, :] = x4.astype(BF16)
        for r in range(16):
            x16 = x4f_ref[pl.ds((r % 4) * sub4 + r // 4, sub16, stride=4), :]
            x16_ref[r * sub16:(r + 1) * sub16, :] = x16.astype(BF16)

    d1 = (q1_ref, k1_ref, v1_ref)
    d4 = (q4_ref, k4_ref, v4_ref)
    d16 = (q16_ref, k16_ref, v16_ref)

    def run(blocks, dests):
        for (o, lse), (o_out, l_out, pos, stride) in zip(_attend_blocks(blocks), dests):
            store(o_out, l_out, pos, stride, o, lse)

    run([(*d1, 0, 0)] + [(*d4, r * sub4, 0) for r in range(4)],
        [(o1_ref, l1_ref, 0, 1)] + [(o4_ref, l4_ref, r, 4) for r in range(4)])

    group16 = 16

    def d16_body(i, carry):
        rs = [i * group16 + u for u in range(group16)]
        run([(*d16, pl.multiple_of(r * sub16, BLK), 0) for r in rs],
            [(o16_ref, l16_ref, r, 16) for r in rs])
        return carry

    lax.fori_loop(0, 16 // group16, d16_body, 0)

    group1 = 15

    def d1_body(i, carry):
        ns = [1 + i * group1 + u for u in range(group1)]
        run([(*d1, pl.multiple_of(n * BLK, BLK), BLK) for n in ns],
            [(o1_ref, l1_ref, n * BLK, 1) for n in ns])
        return carry

    lax.fori_loop(0, (seq // BLK - 1) // group1, d1_body, 0)

    rn = [(r, n) for n in range(1, sub4 // BLK) for r in range(4)]
    run([(*d4, r * sub4 + n * BLK, BLK) for r, n in rn],
        [(o4_ref, l4_ref, r + n * BLK * 4, 4) for r, n in rn])

    rows = 256

    def merge_body(c, carry):
        sl = pl.ds(pl.multiple_of(c * rows, rows), rows)
        l1, l4, l16 = l1_ref[sl, :], l4_ref[sl, :], l16_ref[sl, :]
        m = jnp.maximum(jnp.maximum(l1, l4), l16)
        e1, e4, e16 = jnp.exp(l1 - m), jnp.exp(l4 - m), jnp.exp(l16 - m)
        num = e1 * o1_ref[sl, :] + e4 * o4_ref[sl, :] + e16 * o16_ref[sl, :]
        o_ref[sl, :] = (num / (e1 + e4 + e16)).astype(o_ref.dtype)
        return carry

    lax.fori_loop(0, seq // rows, merge_body, 0)


def _attn_prompt(qa, ka, va):
    bn, seq, _ = qa.shape
    assert seq == A_BUF, "prompt attention is written for a sequence of exactly the largest window"
    spec = pl.BlockSpec((None, seq, HEAD_DIM), lambda b, h: (b, 0, h))
    return pl.pallas_call(
        functools.partial(_attn_prompt_kernel, seq=seq),
        grid=(bn, A_HEADS),
        in_specs=[spec, spec, spec],
        out_specs=spec,
        out_shape=jax.ShapeDtypeStruct((bn, seq, A_WIDTH), BF16),
        scratch_shapes=[pltpu.VMEM((seq, HEAD_DIM), BF16)] * 9 + [pltpu.VMEM((seq, LANES), F32)] * 9,
        compiler_params=_params(("parallel", "parallel")),
        name="attn_prompt",
    )(qa, ka, va)


Q_ROWS = 8
NEAR = 4 * KEYS_BACK


def _attn_sample_kernel(q_ref, kn_ref, vn_ref, nk_ref, nv_ref, fk_ref, fv_ref, o_ref, *, n_new):
    scale = HEAD_DIM ** -0.5
    ext = NEAR + LANES
    total = ext + n_new * KEYS_BACK

    t = lax.broadcasted_iota(jnp.int32, (Q_ROWS, total), 0) & 3
    c = lax.broadcasted_iota(jnp.int32, (Q_ROWS, total), 1)
    c1 = c - (NEAR - KEYS_BACK)
    in1 = (c1 >= t) & (c1 <= KEYS_BACK + t)
    in4 = ((c & 3) == t) & (c <= NEAR + t)
    in16 = ((c - ext) >> 7) == t
    mult = in1.astype(F32) + in4.astype(F32) + in16.astype(F32)

    def rows_of(near_ref, far_ref, new, h):
        near = near_ref[pl.ds(h, NEAR, stride=A_HEADS), :]
        far = [far_ref[:, tt * A_HEADS + h, :] for tt in range(n_new)]
        pad = jnp.zeros((LANES - Q_ROWS, HEAD_DIM), F32)
        return jnp.concatenate([near, new, pad] + far, axis=0).astype(BF16)

    heads = range(A_HEADS)
    sl = lambda h: slice(h * HEAD_DIM, (h + 1) * HEAD_DIM)
    qs = [q_ref[:, sl(h)].astype(BF16) for h in heads]
    kns = [kn_ref[:, sl(h)] for h in heads]
    vns = [vn_ref[:, sl(h)] for h in heads]
    scores = [_dot_nt(qs[h], rows_of(nk_ref, fk_ref, kns[h], h)) * scale for h in heads]
    probs = []
    for h in heads:
        s_self = jnp.sum(qs[h].astype(F32) * kns[h].astype(BF16).astype(F32), axis=-1, keepdims=True) * scale
        s = jnp.where(mult > 0.0, scores[h], NEG_INF)
        m = jnp.maximum(jnp.max(s, axis=-1, keepdims=True), s_self)
        p = jnp.exp(s - m) * mult
        p_self = jnp.exp(s_self - m)
        probs.append((p.astype(BF16), p_self, jnp.sum(p, axis=-1, keepdims=True) + p_self))
    for h in heads:
        p, p_self, l = probs[h]
        acc = _dot(p, rows_of(nv_ref, fv_ref, vns[h], h))
        acc = acc + p_self.astype(BF16).astype(F32) * vns[h].astype(BF16).astype(F32)
        o_ref[:, sl(h)] = acc / l


def _attn_sample(qn, kn, vn, cache_k, cache_v, n_new):
    bn, cache_len = cache_k.shape[:2]
    assert KEYS_BACK == 128 and cache_len == A_BUF and n_new == 4, "written for a full window cache and 4 new tokens"
    new_spec = pl.BlockSpec((None, Q_ROWS, A_WIDTH), lambda b: (b, 0, 0))
    near_spec = pl.BlockSpec((None, NEAR * A_HEADS, HEAD_DIM), lambda b: (b, cache_len // NEAR - 1, 0))
    far_spec = pl.BlockSpec((None, KEYS_BACK, n_new * A_HEADS, HEAD_DIM), lambda b: (b, 0, 0, 0))
    near = lambda c: c.reshape(bn, cache_len * A_HEADS, HEAD_DIM)
    far = lambda c: c.reshape(bn, KEYS_BACK, 16 * A_HEADS, HEAD_DIM)
    return pl.pallas_call(
        functools.partial(_attn_sample_kernel, n_new=n_new),
        grid=(bn,),
        in_specs=[new_spec, new_spec, new_spec, near_spec, near_spec, far_spec, far_spec],
        out_specs=new_spec,
        out_shape=jax.ShapeDtypeStruct((bn, Q_ROWS, A_WIDTH), F32),
        compiler_params=_params(("parallel",)),
        name="attn_sample",
    )(qn, kn, vn, near(cache_k), near(cache_v), far(cache_k), far(cache_v))


def _gla_kernel(q_ref, k_ref, lb_ref, v_ref, rb_ref, g_ref, s0_ref, o_ref, sfin_ref, st_ref,
                *, n_chunks, group, heads):
    c = GLA_CHUNK
    for hd in range(heads):
        st_ref[hd] = s0_ref[hd].T
    ti = lax.broadcasted_iota(jnp.int32, (c, c), 0)
    tj = lax.broadcasted_iota(jnp.int32, (c, c), 1)
    causal = tj <= ti

    def group_body(gi, carry):
        rows = [pl.ds(pl.multiple_of((gi * group + u) * c, c), c) for u in range(group)]
        items = [(hd, r) for hd in range(heads) for r in rows]
        dk = lambda hd: slice(hd * B_DK, (hd + 1) * B_DK)
        dv = lambda hd: slice(hd * B_DV, (hd + 1) * B_DV)
        q_dec, k_inv, k_end, decay = [], [], [], []
        for hd, r in items:
            b = lb_ref[r, dk(hd)]
            b_last = b[c - 1:c, :]
            q = q_ref[r, dk(hd)]
            k = k_ref[r, dk(hd)]
            q_dec.append((q * jnp.exp(b)).astype(BF16))
            k_inv.append((k * jnp.exp(-b)).astype(BF16))
            k_end.append((k * jnp.exp(b_last - b)).astype(BF16))
            decay.append(jnp.exp(b_last))
        att = [jnp.where(causal, _dot_nt(qd, ki), 0.0).astype(BF16) for qd, ki in zip(q_dec, k_inv)]
        delta = [_dot_tn(v_ref[r, dv(hd)], ke) for (hd, r), ke in zip(items, k_end)]
        before = []
        for hd in range(heads):
            st = st_ref[hd]
            for i in range(hd * group, (hd + 1) * group):
                before.append(st.astype(BF16))
                st = st * decay[i] + delta[i]
            st_ref[hd] = st
        for (hd, r), a, qd, sb in zip(items, att, q_dec, before):
            o = _dot(a, v_ref[r, dv(hd)]) + _dot_nt(qd, sb)
            gated = _rms_scale(o, g_ref[hd]) * jax.nn.silu(rb_ref[r, dv(hd)])
            o_ref[r, dv(hd)] = gated.astype(o_ref.dtype)
        return carry

    lax.fori_loop(0, n_chunks // group, group_body, 0)
    for hd in range(heads):
        sfin_ref[hd] = st_ref[hd].T


def _gla(qkb, lb, vb, rb, g, s0, out_dtype):
    bn, t, _ = qkb.shape
    assert t % GLA_CHUNK == 0
    n_chunks = t // GLA_CHUNK
    heads, group = (2, 8) if n_chunks % 8 == 0 else (B_HEADS, 1)
    nhb = B_HEADS // heads
    return pl.pallas_call(
        functools.partial(_gla_kernel, n_chunks=n_chunks, group=group, heads=heads),
        grid=(bn, nhb),
        in_specs=[
            pl.BlockSpec((None, t, heads * B_DK), lambda b, h: (b, 0, h)),
            pl.BlockSpec((None, t, heads * B_DK), lambda b, h: (b, 0, nhb + h)),
            pl.BlockSpec((None, t, heads * B_DK), lambda b, h: (b, 0, h)),
            pl.BlockSpec((None, t, heads * B_DV), lambda b, h: (b, 0, h)),
            pl.BlockSpec((None, t, heads * B_DV), lambda b, h: (b, 0, h)),
            pl.BlockSpec((heads, 1, B_DV), lambda b, h: (h, 0, 0)),
            pl.BlockSpec((None, heads, B_DK, B_DV), lambda b, h: (b, h, 0, 0)),
        ],
        out_specs=(
            pl.BlockSpec((None, t, heads * B_DV), lambda b, h: (b, 0, h)),
            pl.BlockSpec((None, heads, B_DK, B_DV), lambda b, h: (b, h, 0, 0)),
        ),
        out_shape=(
            jax.ShapeDtypeStruct((bn, t, B_WIDTH), out_dtype),
            jax.ShapeDtypeStruct((bn, B_HEADS, B_DK, B_DV), F32),
        ),
        scratch_shapes=[pltpu.VMEM((heads, B_DV, B_DK), F32)],
        compiler_params=_params(("parallel", "parallel")),
        name="gla",
    )(qkb, qkb, lb, vb, rb, g, s0)


def _outproj_kernel(x_ref, ma_ref, mb_ref, wa_ref, wb_ref, g_ref, x1_ref, hm_ref):
    x1 = x_ref[...] + _dot(ma_ref[...].astype(BF16), wa_ref[...]) + _dot(mb_ref[...].astype(BF16), wb_ref[...])
    x1_ref[...] = x1
    hm_ref[...] = _rms_scale(x1, g_ref[...]).astype(BF16)


def _outproj(x, mix_a, mix_b, w_a, w_b, g, tm):
    t, d = x.shape
    row = lambda i: (i, 0)
    const = lambda i: (0, 0)
    return pl.pallas_call(
        _outproj_kernel,
        grid=(t // tm,),
        in_specs=[
            pl.BlockSpec((tm, d), row),
            pl.BlockSpec((tm, A_WIDTH), row),
            pl.BlockSpec((tm, B_WIDTH), row),
            pl.BlockSpec((A_WIDTH, d), const),
            pl.BlockSpec((B_WIDTH, d), const),
            pl.BlockSpec((1, d), const),
        ],
        out_specs=(pl.BlockSpec((tm, d), row), pl.BlockSpec((tm, d), row)),
        out_shape=(jax.ShapeDtypeStruct((t, d), F32), jax.ShapeDtypeStruct((t, d), BF16)),
        compiler_params=_params(("parallel",)),
        name="outproj",
    )(x, mix_a, mix_b, w_a, w_b, g)


def _mlp_kernel(x1_ref, hm_ref, wu_ref, wd_ref, y_ref):
    f = pl.program_id(1)

    @pl.when(f == 0)
    def _():
        y_ref[...] = x1_ref[...]

    u = jnp.square(jnp.maximum(_dot(hm_ref[...], wu_ref[...]), 0.0))
    y_ref[...] += _dot(u.astype(BF16), wd_ref[...])


def _mlp(x1, hm, w_up, w_down, tm, tf):
    t, d = x1.shape
    d_ff = w_up.shape[1]
    row = lambda i, f: (i, 0)
    return pl.pallas_call(
        _mlp_kernel,
        grid=(t // tm, d_ff // tf),
        in_specs=[
            pl.BlockSpec((tm, d), row),
            pl.BlockSpec((tm, d), row),
            pl.BlockSpec((d, tf), lambda i, f: (0, f)),
            pl.BlockSpec((tf, d), lambda i, f: (f, 0)),
        ],
        out_specs=pl.BlockSpec((tm, d), row),
        out_shape=jax.ShapeDtypeStruct((t, d), F32),
        compiler_params=_params(("parallel", "arbitrary")),
        name="mlp",
    )(x1, hm, w_up, w_down)


def _row_tile(t):
    return 512 if t % 512 == 0 else t


def _layer_weights(attn_norm_g, w_in, q_norm_g, k_norm_g, w_gate2, b_gate, gla_norm_g, w_out, mlp_norm_g,
                   w_up, w_down):
    w_g2 = jnp.pad(w_gate2, ((0, LANES - GATE_RANK), (0, 0)))
    return dict(
        attn_g=attn_norm_g[None, :],
        w_all=jnp.pad(w_in.astype(BF16), ((0, 0), (0, LANES - GATE_RANK))),
        w_g2=w_g2.astype(BF16),
        b_gate=b_gate[None, :],
        q_g=q_norm_g[None, :],
        k_g=k_norm_g[None, :],
        gla_g=gla_norm_g[:, None, :],
        w_out_a=w_out[:A_WIDTH].astype(BF16),
        w_out_b=w_out[A_WIDTH:].astype(BF16),
        mlp_g=mlp_norm_g[None, :],
        w_up=w_up.astype(BF16),
        w_down=w_down.astype(BF16),
    )


def _mixer_inputs(x2d, w, chunk):
    t = x2d.shape[0]
    tm = 256 if t % 256 == 0 else t
    return _inproj(x2d, w["attn_g"], w["w_all"], w["w_g2"], w["b_gate"], w["q_g"], w["k_g"], tm, chunk)


def _finish(x2d, mix_a, mix_b, w):
    t = x2d.shape[0]
    x1, hm = _outproj(x2d, mix_a, mix_b, w["w_out_a"], w["w_out_b"], w["mlp_g"], _row_tile(t))
    return _mlp(x1, hm, w["w_up"], w["w_down"], _row_tile(t), 1024)


def _prompt_layer(xp, w):
    bn, seq, d = xp.shape
    x2d = xp.reshape(bn * seq, d)
    assert seq % GLA_CHUNK == 0, "GLA chunks must not straddle batch rows"
    qa, ka, va, kwin, vwin, qkb, vb, rb, lb = _mixer_inputs(x2d, w, GLA_CHUNK)
    b3 = lambda a: a.reshape(bn, seq, a.shape[-1])
    oa = _attn_prompt(b3(qa), b3(ka), b3(va))
    s0 = jnp.zeros((bn, B_HEADS, B_DK, B_DV), F32)
    ob, s_fin = _gla(b3(qkb), b3(lb), b3(vb), b3(rb), w["gla_g"], s0, BF16)
    y = _finish(x2d, oa.reshape(bn * seq, A_WIDTH), ob.reshape(bn * seq, B_WIDTH), w)
    lp = min(A_BUF, seq)
    win_k = kwin.reshape(bn, seq, A_HEADS, HEAD_DIM)[:, seq - lp:]
    win_v = vwin.reshape(bn, seq, A_HEADS, HEAD_DIM)[:, seq - lp:]
    return y.reshape(bn, seq, d), win_k, win_v, s_fin


def _sample_layer(xs, cache_k, cache_v, state, w):
    bn, n_new, d = xs.shape
    x2d = xs.reshape(bn * n_new, d)
    assert n_new <= GLA_CHUNK, "the new tokens of one batch row form a single GLA chunk"
    qa, ka, va, kwin, vwin, qkb, vb, rb, lb = _mixer_inputs(x2d, w, n_new)

    def padded(a, rows, mode="constant"):
        a = a.reshape(bn, n_new, a.shape[-1])
        return jnp.pad(a, ((0, 0), (0, rows - n_new), (0, 0)), mode=mode)

    oa = _attn_sample(padded(qa, Q_ROWS), padded(ka, Q_ROWS), padded(va, Q_ROWS), cache_k, cache_v, n_new)
    c = GLA_CHUNK
    ob, s_new = _gla(padded(qkb, c), padded(lb, c, "edge"), padded(vb, c), padded(rb, c), w["gla_g"], state, F32)
    y = _finish(x2d, oa[:, :n_new].reshape(bn * n_new, A_WIDTH), ob[:, :n_new].reshape(bn * n_new, B_WIDTH), w)
    win_k = kwin.reshape(bn, n_new, A_HEADS, HEAD_DIM)
    win_v = vwin.reshape(bn, n_new, A_HEADS, HEAD_DIM)
    return y.reshape(bn, n_new, d), win_k, win_v, s_new


def kernel(x_prompt, x_sample, cache_win_k, cache_win_v, state_gla, attn_norm_g, w_in, q_norm_g, k_norm_g,
           w_gate2, b_gate, gla_norm_g, w_out, mlp_norm_g, w_up, w_down):
    depth = w_in.shape[0]
    xp, xs = x_prompt, x_sample
    pk, pv, ps, sk, sv, ss = [], [], [], [], [], []
    for l in range(depth):
        w = _layer_weights(attn_norm_g[l], w_in[l], q_norm_g[l], k_norm_g[l], w_gate2[l], b_gate[l],
                           gla_norm_g[l], w_out[l], mlp_norm_g[l], w_up[l], w_down[l])
        xp, k_l, v_l, s_l = _prompt_layer(xp, w)
        pk.append(k_l)
        pv.append(v_l)
        ps.append(s_l)
        xs, k_l, v_l, s_l = _sample_layer(xs, cache_win_k[l], cache_win_v[l], state_gla[l], w)
        sk.append(k_l)
        sv.append(v_l)
        ss.append(s_l)
    return (xp, xs, jnp.stack(pk), jnp.stack(pv), jnp.stack(ps), jnp.stack(sk), jnp.stack(sv), jnp.stack(ss))
```

```python
import functools

import jax
import jax.numpy as jnp
from jax import lax
from jax.experimental import pallas as pl
from jax.experimental.pallas import tpu as pltpu

F32 = jnp.float32
BF16 = jnp.bfloat16

HEAD_DIM = 128
A_HEADS = 8
A_WIDTH = A_HEADS * HEAD_DIM
DILATIONS = (1, 4, 16)
KEYS_BACK = 128
A_BUF = KEYS_BACK * max(DILATIONS)
B_HEADS = 4
B_DK = 128
B_DV = 256
B_WIDTH = B_HEADS * B_DV
GATE_RANK = 16
GATE_TAU = 16.0
GLA_CHUNK = 64
RMS_EPS = 1e-6
LANES = 128
NEG_INF = float("-inf")

VMEM_LIMIT_BYTES = 56 * 1024 * 1024


def _params(semantics):
    return pltpu.CompilerParams(dimension_semantics=semantics, vmem_limit_bytes=VMEM_LIMIT_BYTES)


def _dot(a, b):
    return jnp.dot(a, b, preferred_element_type=F32)


def _dot_nt(a, b):
    return lax.dot_general(a, b, (((1,), (1,)), ((), ())), preferred_element_type=F32)


def _dot_tn(a, b):
    return lax.dot_general(a, b, (((0,), (0,)), ((), ())), preferred_element_type=F32)


def _rms_scale(x, g):
    return x * lax.rsqrt(jnp.mean(x * x, axis=-1, keepdims=True) + RMS_EPS) * g


N_MAIN_TILES = 6


CUMSUM_ROWS = 128


def _inproj_kernel(x_ref, g_ref, w_ref, wg2_ref, bg_ref, qg_ref, kg_ref,
                   qa_ref, ka_ref, va_ref, kwin_ref, vwin_ref, qkb_ref, vb_ref, rb_ref, lb_ref, hn_ref,
                   *, chunk):
    tm = x_ref.shape[0]
    hn_ref[...] = _rms_scale(x_ref[...], g_ref[...]).astype(BF16)

    def proj(j):
        return _dot(hn_ref[...], w_ref[:, j * A_WIDTH:(j + 1) * A_WIDTH])

    def per_head(y, fn, out_ref, win_ref):
        for h in range(A_HEADS):
            sl = slice(h * HEAD_DIM, (h + 1) * HEAD_DIM)
            yh = fn(y[:, sl])
            if len(out_ref.shape) == 3:
                out_ref[h] = yh
            else:
                out_ref[:, sl] = yh
            if win_ref is not None:
                win_ref[pl.ds(h, tm, stride=A_HEADS), :] = yh

    n_main = N_MAIN_TILES * A_WIDTH
    ga = _dot(hn_ref[...], w_ref[:, n_main:])
    z = _dot(ga.astype(BF16), wg2_ref[...]) + bg_ref[...]
    log_a = (jnp.minimum(z, 0.0) - jnp.log1p(jnp.exp(-jnp.abs(z)))) * (1.0 / GATE_TAU)

    per_head(proj(0), lambda y: _rms_scale(y, qg_ref[...]), qa_ref, None)

    shift = chunk.bit_length() - 1
    ti = lax.broadcasted_iota(jnp.int32, (CUMSUM_ROWS, CUMSUM_ROWS), 0)
    tj = lax.broadcasted_iota(jnp.int32, (CUMSUM_ROWS, CUMSUM_ROWS), 1)
    tri = (((ti >> shift) == (tj >> shift)) & (tj <= ti)).astype(F32)
    for r in range(tm // CUMSUM_ROWS):
        rows = slice(r * CUMSUM_ROWS, (r + 1) * CUMSUM_ROWS)
        lb_ref[rows, :] = jnp.dot(tri, log_a[rows, :], preferred_element_type=F32,
                                  precision=lax.Precision.HIGHEST)

    per_head(proj(1), lambda y: _rms_scale(y, kg_ref[...]), ka_ref, kwin_ref)
    per_head(proj(2), lambda y: y, va_ref, vwin_ref)

    y = proj(3)
    half = B_HEADS * B_DK
    qkb_ref[:, :half] = y[:, :half] * (B_DK ** -0.5)
    qkb_ref[:, half:] = y[:, half:]

    vb_ref[...] = proj(4).astype(BF16)
    rb_ref[...] = proj(5)


def _inproj(x, g, w_all, w_g2, b_gate, q_g, k_g, tm, chunk, seq=None):
    t, d = x.shape
    assert chunk & (chunk - 1) == 0 and CUMSUM_ROWS % chunk == 0 and tm % CUMSUM_ROWS == 0
    assert w_all.shape == (d, N_MAIN_TILES * A_WIDTH + LANES)
    row = lambda i: (i, 0)
    const = lambda i: (0, 0)
    resident = dict(pipeline_mode=pl.Buffered(1))
    if seq is None:
        qkv_shape = jax.ShapeDtypeStruct((t, A_WIDTH), F32)
        qkv_spec = pl.BlockSpec((tm, A_WIDTH), row)
    else:
        assert seq % tm == 0
        tiles = seq // tm
        qkv_shape = jax.ShapeDtypeStruct((t // seq, A_HEADS, seq, HEAD_DIM), F32)
        qkv_spec = pl.BlockSpec((None, A_HEADS, tm, HEAD_DIM), lambda i: (i // tiles, 0, i % tiles, 0))
    out_shapes = (
        qkv_shape,
        qkv_shape,
        qkv_shape,
        jax.ShapeDtypeStruct((t * A_HEADS, HEAD_DIM), F32),
        jax.ShapeDtypeStruct((t * A_HEADS, HEAD_DIM), F32),
        jax.ShapeDtypeStruct((t, 2 * B_HEADS * B_DK), F32),
        jax.ShapeDtypeStruct((t, B_WIDTH), BF16),
        jax.ShapeDtypeStruct((t, B_WIDTH), F32),
        jax.ShapeDtypeStruct((t, B_HEADS * B_DK), F32),
    )
    return pl.pallas_call(
        functools.partial(_inproj_kernel, chunk=chunk),
        grid=(t // tm,),
        in_specs=[
            pl.BlockSpec((tm, d), row),
            pl.BlockSpec((1, d), const),
            pl.BlockSpec(w_all.shape, const, **resident),
            pl.BlockSpec((LANES, B_HEADS * B_DK), const),
            pl.BlockSpec((1, B_HEADS * B_DK), const),
            pl.BlockSpec((1, HEAD_DIM), const),
            pl.BlockSpec((1, HEAD_DIM), const),
        ],
        out_specs=(qkv_spec,) * 3 + tuple(pl.BlockSpec((tm * s.shape[0] // t, s.shape[1]), row)
                                          for s in out_shapes[3:]),
        out_shape=out_shapes,
        scratch_shapes=[pltpu.VMEM((tm, d), BF16)],
        compiler_params=_params(("parallel",)),
        name="inproj",
    )(x, g, w_all, w_g2, b_gate, q_g, k_g)


BLK = KEYS_BACK


def _attend_blocks(blocks):
    scores = []
    for q_ref, k_ref, _, start, back in blocks:
        q = q_ref[pl.ds(start, BLK), :]
        k = k_ref[pl.ds(start - back, BLK + back), :]
        s = _dot_nt(q, k) * (HEAD_DIM ** -0.5)
        qi = lax.broadcasted_iota(jnp.int32, s.shape, 0)
        kj = lax.broadcasted_iota(jnp.int32, s.shape, 1)
        valid = (kj >= qi) & (kj <= qi + BLK) if back else kj <= qi
        scores.append(jnp.where(valid, s, NEG_INF))
    probs = []
    for s in scores:
        m = jnp.max(s, axis=-1, keepdims=True)
        p = jnp.exp(s - m)
        probs.append((p.astype(BF16), m, jnp.sum(p, axis=-1, keepdims=True)))
    outs = []
    for (p, m, l), (_, _, v_ref, start, back) in zip(probs, blocks):
        v = v_ref[pl.ds(start - back, BLK + back), :]
        outs.append((_dot(p, v) / l, m + jnp.log(l)))
    return outs


def _attn_prompt_kernel(q_ref, k_ref, v_ref, o_ref,
                        q1_ref, k1_ref, v1_ref, q4_ref, k4_ref, v4_ref, q16_ref, k16_ref, v16_ref,
                        q4f_ref, k4f_ref, v4f_ref,
                        o1_ref, l1_ref, o4_ref, l4_ref, o16_ref, l16_ref,
                        *, seq):
    def store(o_out, l_out, pos, stride, o, lse):
        if stride == 1:
            idx = pl.ds(pl.multiple_of(pos, BLK), BLK)
        else:
            idx = pl.ds(pos, BLK, stride=stride)
        o_out[idx, :] = o
        l_out[idx, :] = jnp.broadcast_to(lse, (BLK, LANES))

    sub4, sub16 = seq // 4, seq // 16
    for x_ref, x1_ref, x4_ref, x4f_ref, x16_ref in ((q_ref, q1_ref, q4_ref, q4f_ref, q16_ref),
                                                   (k_ref, k1_ref, k4_ref, k4f_ref, k16_ref),
                                                   (v_ref, v1_ref, v4_ref, v4f_ref, v16_ref)):
        x1_ref[...] = x_ref[...].astype(BF16)
        for r in range(4):
            x4 = x_ref[pl.ds(r, sub4, stride=4), :]
            x4f_ref[r * sub4:(r + 1) * sub4, :] = x4
            x4_ref[r * sub4:(r + 1) * sub4, :] = x4.astype(BF16)
        for r in range(16):
            x16 = x4f_ref[pl.ds((r % 4) * sub4 + r // 4, sub16, stride=4), :]
            x16_ref[r * sub16:(r + 1) * sub16, :] = x16.astype(BF16)

    d1 = (q1_ref, k1_ref, v1_ref)
    d4 = (q4_ref, k4_ref, v4_ref)
    d16 = (q16_ref, k16_ref, v16_ref)

    def run(blocks, dests):
        for (o, lse), (o_out, l_out, pos, stride) in zip(_attend_blocks(blocks), dests):
            store(o_out, l_out, pos, stride, o, lse)

    run([(*d1, 0, 0)] + [(*d4, r * sub4, 0) for r in range(4)],
        [(o1_ref, l1_ref, 0, 1)] + [(o4_ref, l4_ref, r, 4) for r in range(4)])

    group16 = 16

    def d16_body(i, carry):
        rs = [i * group16 + u for u in range(group16)]
        run([(*d16, pl.multiple_of(r * sub16, BLK), 0) for r in rs],
            [(o16_ref, l16_ref, r, 16) for r in rs])
        return carry

    lax.fori_loop(0, 16 // group16, d16_body, 0)

    group1 = 15

    def d1_body(i, carry):
        ns = [1 + i * group1 + u for u in range(group1)]
        run([(*d1, pl.multiple_of(n * BLK, BLK), BLK) for n in ns],
            [(o1_ref, l1_ref, n * BLK, 1) for n in ns])
        return carry

    lax.fori_loop(0, (seq // BLK - 1) // group1, d1_body, 0)

    rn = [(r, n) for n in range(1, sub4 // BLK) for r in range(4)]
    run([(*d4, r * sub4 + n * BLK, BLK) for r, n in rn],
        [(o4_ref, l4_ref, r + n * BLK * 4, 4) for r, n in rn])

    rows = 256

    def merge_body(c, carry):
        sl = pl.ds(pl.multiple_of(c * rows, rows), rows)
        l1, l4, l16 = l1_ref[sl, :], l4_ref[sl, :], l16_ref[sl, :]
        m = jnp.maximum(jnp.maximum(l1, l4), l16)
        e1, e4, e16 = jnp.exp(l1 - m), jnp.exp(l4 - m), jnp.exp(l16 - m)
        num = e1 * o1_ref[sl, :] + e4 * o4_ref[sl, :] + e16 * o16_ref[sl, :]
        o_ref[sl, :] = (num / (e1 + e4 + e16)).astype(o_ref.dtype)
        return carry

    lax.fori_loop(0, seq // rows, merge_body, 0)


def _attn_prompt(qa, ka, va):
    bn, _, seq, _ = qa.shape
    assert seq == A_BUF, "prompt attention is written for a sequence of exactly the largest window"
    assert (seq // BLK - 1) % 15 == 0
    spec = pl.BlockSpec((None, None, seq, HEAD_DIM), lambda b, h: (b, h, 0, 0))
    return pl.pallas_call(
        functools.partial(_attn_prompt_kernel, seq=seq),
        grid=(bn, A_HEADS),
        in_specs=[spec, spec, spec],
        out_specs=pl.BlockSpec((None, seq, HEAD_DIM), lambda b, h: (b, 0, h)),
        out_shape=jax.ShapeDtypeStruct((bn, seq, A_WIDTH), BF16),
        scratch_shapes=[pltpu.VMEM((seq, HEAD_DIM), BF16)] * 9 + [pltpu.VMEM((seq, LANES), F32)] * 9,
        compiler_params=_params(("parallel", "parallel")),
        name="attn_prompt",
    )(qa, ka, va)


Q_ROWS = 8
NEAR = 4 * KEYS_BACK


def _attn_sample_kernel(q_ref, kn_ref, vn_ref, nk_ref, nv_ref, fk_ref, fv_ref, o_ref, *, n_new):
    scale = HEAD_DIM ** -0.5
    ext = NEAR + LANES
    total = ext + n_new * KEYS_BACK

    t = lax.broadcasted_iota(jnp.int32, (Q_ROWS, total), 0) & 3
    c = lax.broadcasted_iota(jnp.int32, (Q_ROWS, total), 1)
    c1 = c - (NEAR - KEYS_BACK)
    in1 = (c1 >= t) & (c1 <= KEYS_BACK + t)
    in4 = ((c & 3) == t) & (c <= NEAR + t)
    in16 = ((c - ext) >> 7) == t
    mult = in1.astype(F32) + in4.astype(F32) + in16.astype(F32)

    def rows_of(near_ref, far_ref, new, h):
        near = near_ref[pl.ds(h, NEAR, stride=A_HEADS), :]
        far = [far_ref[:, tt * A_HEADS + h, :] for tt in range(n_new)]
        pad = jnp.zeros((LANES - Q_ROWS, HEAD_DIM), F32)
        return jnp.concatenate([near, new, pad] + far, axis=0).astype(BF16)

    heads = range(A_HEADS)
    sl = lambda h: slice(h * HEAD_DIM, (h + 1) * HEAD_DIM)
    qs = [q_ref[:, sl(h)].astype(BF16) for h in heads]
    kns = [kn_ref[:, sl(h)] for h in heads]
    vns = [vn_ref[:, sl(h)] for h in heads]
    scores = [_dot_nt(qs[h], rows_of(nk_ref, fk_ref, kns[h], h)) * scale for h in heads]
    probs = []
    for h in heads:
        s_self = jnp.sum(qs[h].astype(F32) * kns[h].astype(BF16).astype(F32), axis=-1, keepdims=True) * scale
        s = jnp.where(mult > 0.0, scores[h], NEG_INF)
        m = jnp.maximum(jnp.max(s, axis=-1, keepdims=True), s_self)
        p = jnp.exp(s - m) * mult
        p_self = jnp.exp(s_self - m)
        probs.append((p.astype(BF16), p_self, jnp.sum(p, axis=-1, keepdims=True) + p_self))
    for h in heads:
        p, p_self, l = probs[h]
        acc = _dot(p, rows_of(nv_ref, fv_ref, vns[h], h))
        acc = acc + p_self.astype(BF16).astype(F32) * vns[h].astype(BF16).astype(F32)
        o_ref[:, sl(h)] = acc / l


def _attn_sample(qn, kn, vn, cache_k, cache_v, n_new):
    bn, cache_len = cache_k.shape[:2]
    assert KEYS_BACK == 128 and cache_len == A_BUF and n_new == 4, "written for a full window cache and 4 new tokens"
    new_spec = pl.BlockSpec((None, Q_ROWS, A_WIDTH), lambda b: (b, 0, 0))
    near_spec = pl.BlockSpec((None, NEAR * A_HEADS, HEAD_DIM), lambda b: (b, cache_len // NEAR - 1, 0))
    far_spec = pl.BlockSpec((None, KEYS_BACK, n_new * A_HEADS, HEAD_DIM), lambda b: (b, 0, 0, 0))
    near = lambda c: c.reshape(bn, cache_len * A_HEADS, HEAD_DIM)
    far = lambda c: c.reshape(bn, KEYS_BACK, 16 * A_HEADS, HEAD_DIM)
    return pl.pallas_call(
        functools.partial(_attn_sample_kernel, n_new=n_new),
        grid=(bn,),
        in_specs=[new_spec, new_spec, new_spec, near_spec, near_spec, far_spec, far_spec],
        out_specs=new_spec,
        out_shape=jax.ShapeDtypeStruct((bn, Q_ROWS, A_WIDTH), F32),
        compiler_params=_params(("parallel",)),
        name="attn_sample",
    )(qn, kn, vn, near(cache_k), near(cache_v), far(cache_k), far(cache_v))


def _gla_kernel(q_ref, k_ref, lb_ref, v_ref, rb_ref, g_ref, s0_ref, o_ref, sfin_ref, st_ref,
                *, n_chunks, group, heads):
    c = GLA_CHUNK
    for hd in range(heads):
        st_ref[hd] = s0_ref[hd].T
    ti = lax.broadcasted_iota(jnp.int32, (c, c), 0)
    tj = lax.broadcasted_iota(jnp.int32, (c, c), 1)
    causal = tj <= ti

    def group_body(gi, carry):
        rows = [pl.ds(pl.multiple_of((gi * group + u) * c, c), c) for u in range(group)]
        items = [(hd, r) for hd in range(heads) for r in rows]
        dk = lambda hd: slice(hd * B_DK, (hd + 1) * B_DK)
        dv = lambda hd: slice(hd * B_DV, (hd + 1) * B_DV)
        q_dec, k_inv, k_end, decay = [], [], [], []
        for hd, r in items:
            b = lb_ref[r, dk(hd)]
            b_last = b[c - 1:c, :]
            q = q_ref[r, dk(hd)]
            k = k_ref[r, dk(hd)]
            q_dec.append((q * jnp.exp(b)).astype(BF16))
            k_inv.append((k * jnp.exp(-b)).astype(BF16))
            k_end.append((k * jnp.exp(b_last - b)).astype(BF16))
            decay.append(jnp.exp(b_last))
        att = [jnp.where(causal, _dot_nt(qd, ki), 0.0).astype(BF16) for qd, ki in zip(q_dec, k_inv)]
        delta = [_dot_tn(v_ref[r, dv(hd)], ke) for (hd, r), ke in zip(items, k_end)]
        before = []
        for hd in range(heads):
            st = st_ref[hd]
            for i in range(hd * group, (hd + 1) * group):
                before.append(st.astype(BF16))
                st = st * decay[i] + delta[i]
            st_ref[hd] = st
        for (hd, r), a, qd, sb in zip(items, att, q_dec, before):
            o = _dot(a, v_ref[r, dv(hd)]) + _dot_nt(qd, sb)
            gated = _rms_scale(o, g_ref[hd]) * jax.nn.silu(rb_ref[r, dv(hd)])
            o_ref[r, dv(hd)] = gated.astype(o_ref.dtype)
        return carry

    lax.fori_loop(0, n_chunks // group, group_body, 0)
    for hd in range(heads):
        sfin_ref[hd] = st_ref[hd].T


def _gla(qkb, lb, vb, rb, g, s0, out_dtype):
    bn, t, _ = qkb.shape
    assert t % GLA_CHUNK == 0
    n_chunks = t // GLA_CHUNK
    heads, group = (2, 8) if n_chunks % 8 == 0 else (B_HEADS, 1)
    nhb = B_HEADS // heads
    return pl.pallas_call(
        functools.partial(_gla_kernel, n_chunks=n_chunks, group=group, heads=heads),
        grid=(bn, nhb),
        in_specs=[
            pl.BlockSpec((None, t, heads * B_DK), lambda b, h: (b, 0, h)),
            pl.BlockSpec((None, t, heads * B_DK), lambda b, h: (b, 0, nhb + h)),
            pl.BlockSpec((None, t, heads * B_DK), lambda b, h: (b, 0, h)),
            pl.BlockSpec((None, t, heads * B_DV), lambda b, h: (b, 0, h)),
            pl.BlockSpec((None, t, heads * B_DV), lambda b, h: (b, 0, h)),
            pl.BlockSpec((heads, 1, B_DV), lambda b, h: (h, 0, 0)),
            pl.BlockSpec((None, heads, B_DK, B_DV), lambda b, h: (b, h, 0, 0)),
        ],
        out_specs=(
            pl.BlockSpec((None, t, heads * B_DV), lambda b, h: (b, 0, h)),
            pl.BlockSpec((None, heads, B_DK, B_DV), lambda b, h: (b, h, 0, 0)),
        ),
        out_shape=(
            jax.ShapeDtypeStruct((bn, t, B_WIDTH), out_dtype),
            jax.ShapeDtypeStruct((bn, B_HEADS, B_DK, B_DV), F32),
        ),
        scratch_shapes=[pltpu.VMEM((heads, B_DV, B_DK), F32)],
        compiler_params=_params(("parallel", "parallel")),
        name="gla",
    )(qkb, qkb, lb, vb, rb, g, s0)


def _outproj_kernel(x_ref, ma_ref, mb_ref, wa_ref, wb_ref, g_ref, x1_ref, hm_ref):
    x1 = x_ref[...] + _dot(ma_ref[...].astype(BF16), wa_ref[...]) + _dot(mb_ref[...].astype(BF16), wb_ref[...])
    x1_ref[...] = x1
    hm_ref[...] = _rms_scale(x1, g_ref[...]).astype(BF16)


def _outproj(x, mix_a, mix_b, w_a, w_b, g, tm):
    t, d = x.shape
    row = lambda i: (i, 0)
    const = lambda i: (0, 0)
    return pl.pallas_call(
        _outproj_kernel,
        grid=(t // tm,),
        in_specs=[
            pl.BlockSpec((tm, d), row),
            pl.BlockSpec((tm, A_WIDTH), row),
            pl.BlockSpec((tm, B_WIDTH), row),
            pl.BlockSpec((A_WIDTH, d), const),
            pl.BlockSpec((B_WIDTH, d), const),
            pl.BlockSpec((1, d), const),
        ],
        out_specs=(pl.BlockSpec((tm, d), row), pl.BlockSpec((tm, d), row)),
        out_shape=(jax.ShapeDtypeStruct((t, d), F32), jax.ShapeDtypeStruct((t, d), BF16)),
        compiler_params=_params(("parallel",)),
        name="outproj",
    )(x, mix_a, mix_b, w_a, w_b, g)


def _mlp_kernel(x1_ref, hm_ref, wu_ref, wd_ref, y_ref):
    f = pl.program_id(1)

    @pl.when(f == 0)
    def _():
        y_ref[...] = x1_ref[...]

    u = jnp.square(jnp.maximum(_dot(hm_ref[...], wu_ref[...]), 0.0))
    y_ref[...] += _dot(u.astype(BF16), wd_ref[...])


def _mlp(x1, hm, w_up, w_down, tm, tf):
    t, d = x1.shape
    d_ff = w_up.shape[1]
    row = lambda i, f: (i, 0)
    return pl.pallas_call(
        _mlp_kernel,
        grid=(t // tm, d_ff // tf),
        in_specs=[
            pl.BlockSpec((tm, d), row),
            pl.BlockSpec((tm, d), row),
            pl.BlockSpec((d, tf), lambda i, f: (0, f)),
            pl.BlockSpec((tf, d), lambda i, f: (f, 0)),
        ],
        out_specs=pl.BlockSpec((tm, d), row),
        out_shape=jax.ShapeDtypeStruct((t, d), F32),
        compiler_params=_params(("parallel", "arbitrary")),
        name="mlp",
    )(x1, hm, w_up, w_down)


def _row_tile(t):
    return 512 if t % 512 == 0 else t


def _layer_weights(attn_norm_g, w_in, q_norm_g, k_norm_g, w_gate2, b_gate, gla_norm_g, w_out, mlp_norm_g,
                   w_up, w_down):
    w_g2 = jnp.pad(w_gate2, ((0, LANES - GATE_RANK), (0, 0)))
    return dict(
        attn_g=attn_norm_g[None, :],
        w_all=jnp.pad(w_in, ((0, 0), (0, LANES - GATE_RANK))).astype(BF16),
        w_g2=w_g2.astype(BF16),
        b_gate=b_gate[None, :],
        q_g=q_norm_g[None, :],
        k_g=k_norm_g[None, :],
        gla_g=gla_norm_g[:, None, :],
        w_out_a=w_out[:A_WIDTH].astype(BF16),
        w_out_b=w_out[A_WIDTH:].astype(BF16),
        mlp_g=mlp_norm_g[None, :],
        w_up=w_up.astype(BF16),
        w_down=w_down.astype(BF16),
    )


def _mixer_inputs(x2d, w, chunk, seq=None):
    t = x2d.shape[0]
    tm = 256 if t % 256 == 0 else t
    return _inproj(x2d, w["attn_g"], w["w_all"], w["w_g2"], w["b_gate"], w["q_g"], w["k_g"], tm, chunk, seq)


def _finish(x2d, mix_a, mix_b, w):
    t = x2d.shape[0]
    x1, hm = _outproj(x2d, mix_a, mix_b, w["w_out_a"], w["w_out_b"], w["mlp_g"], _row_tile(t))
    return _mlp(x1, hm, w["w_up"], w["w_down"], _row_tile(t), 1024)


def _prompt_layer(xp, w):
    bn, seq, d = xp.shape
    x2d = xp.reshape(bn * seq, d)
    assert seq % GLA_CHUNK == 0, "GLA chunks must not straddle batch rows"
    qa, ka, va, kwin, vwin, qkb, vb, rb, lb = _mixer_inputs(x2d, w, GLA_CHUNK, seq)
    b3 = lambda a: a.reshape(bn, seq, a.shape[-1])
    oa = _attn_prompt(qa, ka, va)
    s0 = jnp.zeros((bn, B_HEADS, B_DK, B_DV), F32)
    ob, s_fin = _gla(b3(qkb), b3(lb), b3(vb), b3(rb), w["gla_g"], s0, BF16)
    y = _finish(x2d, oa.reshape(bn * seq, A_WIDTH), ob.reshape(bn * seq, B_WIDTH), w)
    lp = min(A_BUF, seq)
    win_k = kwin.reshape(bn, seq, A_HEADS, HEAD_DIM)[:, seq - lp:]
    win_v = vwin.reshape(bn, seq, A_HEADS, HEAD_DIM)[:, seq - lp:]
    return y.reshape(bn, seq, d), win_k, win_v, s_fin


def _sample_layer(xs, cache_k, cache_v, state, w):
    bn, n_new, d = xs.shape
    x2d = xs.reshape(bn * n_new, d)
    assert n_new <= GLA_CHUNK, "the new tokens of one batch row form a single GLA chunk"
    qa, ka, va, kwin, vwin, qkb, vb, rb, lb = _mixer_inputs(x2d, w, n_new)

    def padded(a, rows, mode="constant"):
        a = a.reshape(bn, n_new, a.shape[-1])
        return jnp.pad(a, ((0, 0), (0, rows - n_new), (0, 0)), mode=mode)

    oa = _attn_sample(padded(qa, Q_ROWS), padded(ka, Q_ROWS), padded(va, Q_ROWS), cache_k, cache_v, n_new)
    c = GLA_CHUNK
    ob, s_new = _gla(padded(qkb, c), padded(lb, c, "edge"), padded(vb, c), padded(rb, c), w["gla_g"], state, F32)
    y = _finish(x2d, oa[:, :n_new].reshape(bn * n_new, A_WIDTH), ob[:, :n_new].reshape(bn * n_new, B_WIDTH), w)
    win_k = kwin.reshape(bn, n_new, A_HEADS, HEAD_DIM)
    win_v = vwin.reshape(bn, n_new, A_HEADS, HEAD_DIM)
    return y.reshape(bn, n_new, d), win_k, win_v, s_new


def kernel(x_prompt, x_sample, cache_win_k, cache_win_v, state_gla, attn_norm_g, w_in, q_norm_g, k_norm_g,
           w_gate2, b_gate, gla_norm_g, w_out, mlp_norm_g, w_up, w_down):
    depth = w_in.shape[0]
    xp, xs = x_prompt, x_sample
    pk, pv, ps, sk, sv, ss = [], [], [], [], [], []
    for l in range(depth):
        w = _layer_weights(attn_norm_g[l], w_in[l], q_norm_g[l], k_norm_g[l], w_gate2[l], b_gate[l],
                           gla_norm_g[l], w_out[l], mlp_norm_g[l], w_up[l], w_down[l])
        xp, k_l, v_l, s_l = _prompt_layer(xp, w)
        pk.append(k_l)
        pv.append(v_l)
        ps.append(s_l)
        xs, k_l, v_l, s_l = _sample_layer(xs, cache_win_k[l], cache_win_v[l], state_gla[l], w)
        sk.append(k_l)
        sv.append(v_l)
        ss.append(s_l)
    return (xp, xs, jnp.stack(pk), jnp.stack(pv), jnp.stack(ps), jnp.stack(sk), jnp.stack(sv), jnp.stack(ss))
```

```python
import functools

import jax
import jax.numpy as jnp
from jax import lax
from jax.experimental import pallas as pl
from jax.experimental.pallas import tpu as pltpu

F32 = jnp.float32
BF16 = jnp.bfloat16

HEAD_DIM = 128
A_HEADS = 8
A_WIDTH = A_HEADS * HEAD_DIM
DILATIONS = (1, 4, 16)
KEYS_BACK = 128
A_BUF = KEYS_BACK * max(DILATIONS)
B_HEADS = 4
B_DK = 128
B_DV = 256
B_WIDTH = B_HEADS * B_DV
GATE_RANK = 16
GATE_TAU = 16.0
GLA_CHUNK = 64
RMS_EPS = 1e-6
LANES = 128
NEG_INF = float("-inf")

VMEM_LIMIT_BYTES = 56 * 1024 * 1024


def _params(semantics):
    return pltpu.CompilerParams(dimension_semantics=semantics, vmem_limit_bytes=VMEM_LIMIT_BYTES)


def _dot(a, b):
    return jnp.dot(a, b, preferred_element_type=F32)


def _dot_nt(a, b):
    return lax.dot_general(a, b, (((1,), (1,)), ((), ())), preferred_element_type=F32)


def _dot_tn(a, b):
    return lax.dot_general(a, b, (((0,), (0,)), ((), ())), preferred_element_type=F32)


def _rms_scale(x, g):
    return x * lax.rsqrt(jnp.mean(x * x, axis=-1, keepdims=True) + RMS_EPS) * g


N_MAIN_TILES = 6


CUMSUM_ROWS = 128


def _inproj_kernel(x_ref, g_ref, w_ref, wg2_ref, bg_ref, qg_ref, kg_ref,
                   qa_ref, ka_ref, va_ref, kwin_ref, vwin_ref, qkb_ref, vb_ref, rb_ref, lb_ref, hn_ref,
                   *, chunk):
    tm = x_ref.shape[0]
    hn_ref[...] = _rms_scale(x_ref[...], g_ref[...]).astype(BF16)

    def proj(j):
        return _dot(hn_ref[...], w_ref[:, j * A_WIDTH:(j + 1) * A_WIDTH])

    def per_head(y, fn, out_ref, win_ref):
        for h in range(A_HEADS):
            sl = slice(h * HEAD_DIM, (h + 1) * HEAD_DIM)
            yh = fn(y[:, sl])
            out_ref[:, sl] = yh
            if win_ref is not None:
                win_ref[pl.ds(h, tm, stride=A_HEADS), :] = yh

    n_main = N_MAIN_TILES * A_WIDTH
    ga = _dot(hn_ref[...], w_ref[:, n_main:])
    z = _dot(ga.astype(BF16), wg2_ref[...]) + bg_ref[...]
    log_a = (jnp.minimum(z, 0.0) - jnp.log1p(jnp.exp(-jnp.abs(z)))) * (1.0 / GATE_TAU)

    per_head(proj(0), lambda y: _rms_scale(y, qg_ref[...]), qa_ref, None)

    shift = chunk.bit_length() - 1
    ti = lax.broadcasted_iota(jnp.int32, (CUMSUM_ROWS, CUMSUM_ROWS), 0)
    tj = lax.broadcasted_iota(jnp.int32, (CUMSUM_ROWS, CUMSUM_ROWS), 1)
    tri = (((ti >> shift) == (tj >> shift)) & (tj <= ti)).astype(F32)
    for r in range(tm // CUMSUM_ROWS):
        rows = slice(r * CUMSUM_ROWS, (r + 1) * CUMSUM_ROWS)
        lb_ref[rows, :] = jnp.dot(tri, log_a[rows, :], preferred_element_type=F32,
                                  precision=lax.Precision.HIGHEST)

    per_head(proj(1), lambda y: _rms_scale(y, kg_ref[...]), ka_ref, kwin_ref)
    per_head(proj(2), lambda y: y, va_ref, vwin_ref)

    y = proj(3)
    half = B_HEADS * B_DK
    qkb_ref[:, :half] = y[:, :half] * (B_DK ** -0.5)
    qkb_ref[:, half:] = y[:, half:]

    vb_ref[...] = proj(4).astype(BF16)
    rb_ref[...] = proj(5)


def _inproj(x, g, w_all, w_g2, b_gate, q_g, k_g, tm, chunk):
    t, d = x.shape
    assert chunk & (chunk - 1) == 0 and CUMSUM_ROWS % chunk == 0 and tm % CUMSUM_ROWS == 0
    assert w_all.shape == (d, N_MAIN_TILES * A_WIDTH + LANES)
    row = lambda i: (i, 0)
    const = lambda i: (0, 0)
    resident = dict(pipeline_mode=pl.Buffered(1))
    out_shapes = (
        jax.ShapeDtypeStruct((t, A_WIDTH), F32),
        jax.ShapeDtypeStruct((t, A_WIDTH), F32),
        jax.ShapeDtypeStruct((t, A_WIDTH), F32),
        jax.ShapeDtypeStruct((t * A_HEADS, HEAD_DIM), F32),
        jax.ShapeDtypeStruct((t * A_HEADS, HEAD_DIM), F32),
        jax.ShapeDtypeStruct((t, 2 * B_HEADS * B_DK), F32),
        jax.ShapeDtypeStruct((t, B_WIDTH), BF16),
        jax.ShapeDtypeStruct((t, B_WIDTH), F32),
        jax.ShapeDtypeStruct((t, B_HEADS * B_DK), F32),
    )
    return pl.pallas_call(
        functools.partial(_inproj_kernel, chunk=chunk),
        grid=(t // tm,),
        in_specs=[
            pl.BlockSpec((tm, d), row),
            pl.BlockSpec((1, d), const),
            pl.BlockSpec(w_all.shape, const, **resident),
            pl.BlockSpec((LANES, B_HEADS * B_DK), const),
            pl.BlockSpec((1, B_HEADS * B_DK), const),
            pl.BlockSpec((1, HEAD_DIM), const),
            pl.BlockSpec((1, HEAD_DIM), const),
        ],
        out_specs=tuple(pl.BlockSpec((tm * s.shape[0] // t, s.shape[1]), row) for s in out_shapes),
        out_shape=out_shapes,
        scratch_shapes=[pltpu.VMEM((tm, d), BF16)],
        compiler_params=_params(("parallel",)),
        name="inproj",
    )(x, g, w_all, w_g2, b_gate, q_g, k_g)


BLK = KEYS_BACK


def _attend_blocks(blocks):
    scores = []
    for q_ref, k_ref, _, start, back in blocks:
        q = q_ref[pl.ds(start, BLK), :]
        k = k_ref[pl.ds(start - back, BLK + back), :]
        s = _dot_nt(q, k) * (HEAD_DIM ** -0.5)
        qi = lax.broadcasted_iota(jnp.int32, s.shape, 0)
        kj = lax.broadcasted_iota(jnp.int32, s.shape, 1)
        valid = (kj >= qi) & (kj <= qi + BLK) if back else kj <= qi
        scores.append(jnp.where(valid, s, NEG_INF))
    probs = []
    for s in scores:
        m = jnp.max(s, axis=-1, keepdims=True)
        p = jnp.exp(s - m)
        probs.append((p.astype(BF16), m, jnp.sum(p, axis=-1, keepdims=True)))
    outs = []
    for (p, m, l), (_, _, v_ref, start, back) in zip(probs, blocks):
        v = v_ref[pl.ds(start - back, BLK + back), :]
        outs.append((_dot(p, v) / l, m + jnp.log(l)))
    return outs


def _attn_prompt_kernel(q_ref, k_ref, v_ref, o_ref,
                        q1_ref, k1_ref, v1_ref, q4_ref, k4_ref, v4_ref, q16_ref, k16_ref, v16_ref,
                        q4f_ref, k4f_ref, v4f_ref,
                        o1_ref, l1_ref, o4_ref, l4_ref, o16_ref, l16_ref, o16x_ref, l16x_ref,
                        *, seq):
    def store(o_out, l_out, pos, stride, o, lse):
        if stride == 1:
            idx = pl.ds(pl.multiple_of(pos, BLK), BLK)
        else:
            idx = pl.ds(pos, BLK, stride=stride)
        o_out[idx, :] = o
        l_out[idx, :] = jnp.broadcast_to(lse, (BLK, LANES))

    sub4, sub16 = seq // 4, seq // 16
    for x_ref, x1_ref, x4_ref, x4f_ref, x16_ref in ((q_ref, q1_ref, q4_ref, q4f_ref, q16_ref),
                                                   (k_ref, k1_ref, k4_ref, k4f_ref, k16_ref),
                                                   (v_ref, v1_ref, v4_ref, v4f_ref, v16_ref)):
        x1_ref[...] = x_ref[...].astype(BF16)
        for r in range(4):
            x4 = x_ref[pl.ds(r, sub4, stride=4), :]
            x4f_ref[r * sub4:(r + 1) * sub4, :] = x4
            x4_ref[r * sub4:(r + 1) * sub4, :] = x4.astype(BF16)
        for r in range(16):
            x16 = x4f_ref[pl.ds((r % 4) * sub4 + r // 4, sub16, stride=4), :]
            x16_ref[r * sub16:(r + 1) * sub16, :] = x16.astype(BF16)

    d1 = (q1_ref, k1_ref, v1_ref)
    d4 = (q4_ref, k4_ref, v4_ref)
    d16 = (q16_ref, k16_ref, v16_ref)

    def run(blocks, dests):
        for (o, lse), (o_out, l_out, pos, stride) in zip(_attend_blocks(blocks), dests):
            store(o_out, l_out, pos, stride, o, lse)

    run([(*d1, 0, 0)] + [(*d4, r * sub4, 0) for r in range(4)],
        [(o1_ref, l1_ref, 0, 1)] + [(o4_ref, l4_ref, r, 4) for r in range(4)])

    run([(*d16, r * sub16, 0) for r in range(16)],
        [(o16x_ref, l16x_ref, (r % 4) * sub4 + r // 4, 4) for r in range(16)])
    for r in range(4):
        o16_ref[pl.ds(r, sub4, stride=4), :] = o16x_ref[r * sub4:(r + 1) * sub4, :]
        l16_ref[pl.ds(r, sub4, stride=4), :] = l16x_ref[r * sub4:(r + 1) * sub4, :]

    group1 = 15

    def d1_body(i, carry):
        ns = [1 + i * group1 + u for u in range(group1)]
        run([(*d1, pl.multiple_of(n * BLK, BLK), BLK) for n in ns],
            [(o1_ref, l1_ref, n * BLK, 1) for n in ns])
        return carry

    lax.fori_loop(0, (seq // BLK - 1) // group1, d1_body, 0)

    rn = [(r, n) for n in range(1, sub4 // BLK) for r in range(4)]
    run([(*d4, r * sub4 + n * BLK, BLK) for r, n in rn],
        [(o4_ref, l4_ref, r + n * BLK * 4, 4) for r, n in rn])

    rows = 256

    def merge_body(c, carry):
        sl = pl.ds(pl.multiple_of(c * rows, rows), rows)
        l1, l4, l16 = l1_ref[sl, :], l4_ref[sl, :], l16_ref[sl, :]
        m = jnp.maximum(jnp.maximum(l1, l4), l16)
        e1, e4, e16 = jnp.exp(l1 - m), jnp.exp(l4 - m), jnp.exp(l16 - m)
        num = e1 * o1_ref[sl, :] + e4 * o4_ref[sl, :] + e16 * o16_ref[sl, :]
        o_ref[sl, :] = (num / (e1 + e4 + e16)).astype(o_ref.dtype)
        return carry

    lax.fori_loop(0, seq // rows, merge_body, 0)


def _attn_prompt(qa, ka, va):
    bn, seq, _ = qa.shape
    assert seq == A_BUF, "prompt attention is written for a sequence of exactly the largest window"
    spec = pl.BlockSpec((None, seq, HEAD_DIM), lambda b, h: (b, 0, h))
    return pl.pallas_call(
        functools.partial(_attn_prompt_kernel, seq=seq),
        grid=(bn, A_HEADS),
        in_specs=[spec, spec, spec],
        out_specs=spec,
        out_shape=jax.ShapeDtypeStruct((bn, seq, A_WIDTH), BF16),
        scratch_shapes=[pltpu.VMEM((seq, HEAD_DIM), BF16)] * 9 + [pltpu.VMEM((seq, LANES), F32)] * 11,
        compiler_params=_params(("parallel", "parallel")),
        name="attn_prompt",
    )(qa, ka, va)


Q_ROWS = 8
NEAR = 4 * KEYS_BACK


def _attn_sample_kernel(q_ref, kn_ref, vn_ref, nk_ref, nv_ref, fk_ref, fv_ref, o_ref, *, n_new):
    scale = HEAD_DIM ** -0.5
    ext = NEAR + LANES
    total = ext + n_new * KEYS_BACK

    t = lax.broadcasted_iota(jnp.int32, (Q_ROWS, total), 0) & 3
    c = lax.broadcasted_iota(jnp.int32, (Q_ROWS, total), 1)
    c1 = c - (NEAR - KEYS_BACK)
    in1 = (c1 >= t) & (c1 <= KEYS_BACK + t)
    in4 = ((c & 3) == t) & (c <= NEAR + t)
    in16 = ((c - ext) >> 7) == t
    mult = in1.astype(F32) + in4.astype(F32) + in16.astype(F32)

    def rows_of(near_ref, far_ref, new, h):
        near = near_ref[pl.ds(h, NEAR, stride=A_HEADS), :]
        far = [far_ref[:, tt * A_HEADS + h, :] for tt in range(n_new)]
        pad = jnp.zeros((LANES - Q_ROWS, HEAD_DIM), F32)
        return jnp.concatenate([near, new, pad] + far, axis=0).astype(BF16)

    heads = range(A_HEADS)
    sl = lambda h: slice(h * HEAD_DIM, (h + 1) * HEAD_DIM)
    qs = [q_ref[:, sl(h)].astype(BF16) for h in heads]
    kns = [kn_ref[:, sl(h)] for h in heads]
    vns = [vn_ref[:, sl(h)] for h in heads]
    scores = [_dot_nt(qs[h], rows_of(nk_ref, fk_ref, kns[h], h)) * scale for h in heads]
    probs = []
    for h in heads:
        s_self = jnp.sum(qs[h].astype(F32) * kns[h].astype(BF16).astype(F32), axis=-1, keepdims=True) * scale
        s = jnp.where(mult > 0.0, scores[h], NEG_INF)
        m = jnp.maximum(jnp.max(s, axis=-1, keepdims=True), s_self)
        p = jnp.exp(s - m) * mult
        p_self = jnp.exp(s_self - m)
        probs.append((p.astype(BF16), p_self, jnp.sum(p, axis=-1, keepdims=True) + p_self))
    for h in heads:
        p, p_self, l = probs[h]
        acc = _dot(p, rows_of(nv_ref, fv_ref, vns[h], h))
        acc = acc + p_self.astype(BF16).astype(F32) * vns[h].astype(BF16).astype(F32)
        o_ref[:, sl(h)] = acc / l


def _attn_sample(qn, kn, vn, cache_k, cache_v, n_new):
    bn, cache_len = cache_k.shape[:2]
    assert KEYS_BACK == 128 and cache_len == A_BUF and n_new == 4, "written for a full window cache and 4 new tokens"
    new_spec = pl.BlockSpec((None, Q_ROWS, A_WIDTH), lambda b: (b, 0, 0))
    near_spec = pl.BlockSpec((None, NEAR * A_HEADS, HEAD_DIM), lambda b: (b, cache_len // NEAR - 1, 0))
    far_spec = pl.BlockSpec((None, KEYS_BACK, n_new * A_HEADS, HEAD_DIM), lambda b: (b, 0, 0, 0))
    near = lambda c: c.reshape(bn, cache_len * A_HEADS, HEAD_DIM)
    far = lambda c: c.reshape(bn, KEYS_BACK, 16 * A_HEADS, HEAD_DIM)
    return pl.pallas_call(
        functools.partial(_attn_sample_kernel, n_new=n_new),
        grid=(bn,),
        in_specs=[new_spec, new_spec, new_spec, near_spec, near_spec, far_spec, far_spec],
        out_specs=new_spec,
        out_shape=jax.ShapeDtypeStruct((bn, Q_ROWS, A_WIDTH), F32),
        compiler_params=_params(("parallel",)),
        name="attn_sample",
    )(qn, kn, vn, near(cache_k), near(cache_v), far(cache_k), far(cache_v))


def _gla_kernel(q_ref, k_ref, lb_ref, v_ref, rb_ref, g_ref, s0_ref, o_ref, sfin_ref, st_ref,
                *, n_chunks, group, heads):
    c = GLA_CHUNK
    for hd in range(heads):
        st_ref[hd] = s0_ref[hd].T
    ti = lax.broadcasted_iota(jnp.int32, (c, c), 0)
    tj = lax.broadcasted_iota(jnp.int32, (c, c), 1)
    causal = tj <= ti

    def group_body(gi, carry):
        rows = [pl.ds(pl.multiple_of((gi * group + u) * c, c), c) for u in range(group)]
        items = [(hd, r) for hd in range(heads) for r in rows]
        dk = lambda hd: slice(hd * B_DK, (hd + 1) * B_DK)
        dv = lambda hd: slice(hd * B_DV, (hd + 1) * B_DV)
        q_dec, k_inv, k_end, decay = [], [], [], []
        for hd, r in items:
            b = lb_ref[r, dk(hd)]
            b_last = b[c - 1:c, :]
            q = q_ref[r, dk(hd)]
            k = k_ref[r, dk(hd)]
            q_dec.append((q * jnp.exp(b)).astype(BF16))
            k_inv.append((k * jnp.exp(-b)).astype(BF16))
            k_end.append((k * jnp.exp(b_last - b)).astype(BF16))
            decay.append(jnp.exp(b_last))
        att = [jnp.where(causal, _dot_nt(qd, ki), 0.0).astype(BF16) for qd, ki in zip(q_dec, k_inv)]
        delta = [_dot_tn(v_ref[r, dv(hd)], ke) for (hd, r), ke in zip(items, k_end)]
        before = []
        for hd in range(heads):
            st = st_ref[hd]
            for i in range(hd * group, (hd + 1) * group):
                before.append(st.astype(BF16))
                st = st * decay[i] + delta[i]
            st_ref[hd] = st
        for (hd, r), a, qd, sb in zip(items, att, q_dec, before):
            o = _dot(a, v_ref[r, dv(hd)]) + _dot_nt(qd, sb)
            gated = _rms_scale(o, g_ref[hd]) * jax.nn.silu(rb_ref[r, dv(hd)])
            o_ref[r, dv(hd)] = gated.astype(o_ref.dtype)
        return carry

    lax.fori_loop(0, n_chunks // group, group_body, 0)
    for hd in range(heads):
        sfin_ref[hd] = st_ref[hd].T


def _gla(qkb, lb, vb, rb, g, s0, out_dtype):
    bn, t, _ = qkb.shape
    assert t % GLA_CHUNK == 0
    n_chunks = t // GLA_CHUNK
    heads, group = (2, 8) if n_chunks % 8 == 0 else (B_HEADS, 1)
    nhb = B_HEADS // heads
    return pl.pallas_call(
        functools.partial(_gla_kernel, n_chunks=n_chunks, group=group, heads=heads),
        grid=(bn, nhb),
        in_specs=[
            pl.BlockSpec((None, t, heads * B_DK), lambda b, h: (b, 0, h)),
            pl.BlockSpec((None, t, heads * B_DK), lambda b, h: (b, 0, nhb + h)),
            pl.BlockSpec((None, t, heads * B_DK), lambda b, h: (b, 0, h)),
            pl.BlockSpec((None, t, heads * B_DV), lambda b, h: (b, 0, h)),
            pl.BlockSpec((None, t, heads * B_DV), lambda b, h: (b, 0, h)),
            pl.BlockSpec((heads, 1, B_DV), lambda b, h: (h, 0, 0)),
            pl.BlockSpec((None, heads, B_DK, B_DV), lambda b, h: (b, h, 0, 0)),
        ],
        out_specs=(
            pl.BlockSpec((None, t, heads * B_DV), lambda b, h: (b, 0, h)),
            pl.BlockSpec((None, heads, B_DK, B_DV), lambda b, h: (b, h, 0, 0)),
        ),
        out_shape=(
            jax.ShapeDtypeStruct((bn, t, B_WIDTH), out_dtype),
            jax.ShapeDtypeStruct((bn, B_HEADS, B_DK, B_DV), F32),
        ),
        scratch_shapes=[pltpu.VMEM((heads, B_DV, B_DK), F32)],
        compiler_params=_params(("parallel", "parallel")),
        name="gla",
    )(qkb, qkb, lb, vb, rb, g, s0)


def _outproj_kernel(x_ref, ma_ref, mb_ref, wa_ref, wb_ref, g_ref, x1_ref, hm_ref):
    x1 = x_ref[...] + _dot(ma_ref[...].astype(BF16), wa_ref[...]) + _dot(mb_ref[...].astype(BF16), wb_ref[...])
    x1_ref[...] = x1
    hm_ref[...] = _rms_scale(x1, g_ref[...]).astype(BF16)


def _outproj(x, mix_a, mix_b, w_a, w_b, g, tm):
    t, d = x.shape
    row = lambda i: (i, 0)
    const = lambda i: (0, 0)
    return pl.pallas_call(
        _outproj_kernel,
        grid=(t // tm,),
        in_specs=[
            pl.BlockSpec((tm, d), row),
            pl.BlockSpec((tm, A_WIDTH), row),
            pl.BlockSpec((tm, B_WIDTH), row),
            pl.BlockSpec((A_WIDTH, d), const),
            pl.BlockSpec((B_WIDTH, d), const),
            pl.BlockSpec((1, d), const),
        ],
        out_specs=(pl.BlockSpec((tm, d), row), pl.BlockSpec((tm, d), row)),
        out_shape=(jax.ShapeDtypeStruct((t, d), F32), jax.ShapeDtypeStruct((t, d), BF16)),
        compiler_params=_params(("parallel",)),
        name="outproj",
    )(x, mix_a, mix_b, w_a, w_b, g)


def _mlp_kernel(x1_ref, hm_ref, wu_ref, wd_ref, y_ref):
    f = pl.program_id(1)

    @pl.when(f == 0)
    def _():
        y_ref[...] = x1_ref[...]

    u = jnp.square(jnp.maximum(_dot(hm_ref[...], wu_ref[...]), 0.0))
    y_ref[...] += _dot(u.astype(BF16), wd_ref[...])


def _mlp(x1, hm, w_up, w_down, tm, tf):
    t, d = x1.shape
    d_ff = w_up.shape[1]
    row = lambda i, f: (i, 0)
    return pl.pallas_call(
        _mlp_kernel,
        grid=(t // tm, d_ff // tf),
        in_specs=[
            pl.BlockSpec((tm, d), row),
            pl.BlockSpec((tm, d), row),
            pl.BlockSpec((d, tf), lambda i, f: (0, f)),
            pl.BlockSpec((tf, d), lambda i, f: (f, 0)),
        ],
        out_specs=pl.BlockSpec((tm, d), row),
        out_shape=jax.ShapeDtypeStruct((t, d), F32),
        compiler_params=_params(("parallel", "arbitrary")),
        name="mlp",
    )(x1, hm, w_up, w_down)


def _row_tile(t):
    return 512 if t % 512 == 0 else t


def _layer_weights(attn_norm_g, w_in, q_norm_g, k_norm_g, w_gate2, b_gate, gla_norm_g, w_out, mlp_norm_g,
                   w_up, w_down):
    w_g2 = jnp.pad(w_gate2, ((0, LANES - GATE_RANK), (0, 0)))
    return dict(
        attn_g=attn_norm_g[None, :],
        w_all=jnp.pad(w_in.astype(BF16), ((0, 0), (0, LANES - GATE_RANK))),
        w_g2=w_g2.astype(BF16),
        b_gate=b_gate[None, :],
        q_g=q_norm_g[None, :],
        k_g=k_norm_g[None, :],
        gla_g=gla_norm_g[:, None, :],
        w_out_a=w_out[:A_WIDTH].astype(BF16),
        w_out_b=w_out[A_WIDTH:].astype(BF16),
        mlp_g=mlp_norm_g[None, :],
        w_up=w_up.astype(BF16),
        w_down=w_down.astype(BF16),
    )


def _mixer_inputs(x2d, w, chunk):
    t = x2d.shape[0]
    tm = 256 if t % 256 == 0 else t
    return _inproj(x2d, w["attn_g"], w["w_all"], w["w_g2"], w["b_gate"], w["q_g"], w["k_g"], tm, chunk)


def _finish(x2d, mix_a, mix_b, w):
    t = x2d.shape[0]
    x1, hm = _outproj(x2d, mix_a, mix_b, w["w_out_a"], w["w_out_b"], w["mlp_g"], _row_tile(t))
    return _mlp(x1, hm, w["w_up"], w["w_down"], _row_tile(t), 1024)


def _prompt_layer(xp, w):
    bn, seq, d = xp.shape
    x2d = xp.reshape(bn * seq, d)
    assert seq % GLA_CHUNK == 0, "GLA chunks must not straddle batch rows"
    qa, ka, va, kwin, vwin, qkb, vb, rb, lb = _mixer_inputs(x2d, w, GLA_CHUNK)
    b3 = lambda a: a.reshape(bn, seq, a.shape[-1])
    oa = _attn_prompt(b3(qa), b3(ka), b3(va))
    s0 = jnp.zeros((bn, B_HEADS, B_DK, B_DV), F32)
    ob, s_fin = _gla(b3(qkb), b3(lb), b3(vb), b3(rb), w["gla_g"], s0, BF16)
    y = _finish(x2d, oa.reshape(bn * seq, A_WIDTH), ob.reshape(bn * seq, B_WIDTH), w)
    lp = min(A_BUF, seq)
    win_k = kwin.reshape(bn, seq, A_HEADS, HEAD_DIM)[:, seq - lp:]
    win_v = vwin.reshape(bn, seq, A_HEADS, HEAD_DIM)[:, seq - lp:]
    return y.reshape(bn, seq, d), win_k, win_v, s_fin


def _sample_layer(xs, cache_k, cache_v, state, w):
    bn, n_new, d = xs.shape
    x2d = xs.reshape(bn * n_new, d)
    assert n_new <= GLA_CHUNK, "the new tokens of one batch row form a single GLA chunk"
    qa, ka, va, kwin, vwin, qkb, vb, rb, lb = _mixer_inputs(x2d, w, n_new)

    def padded(a, rows, mode="constant"):
        a = a.reshape(bn, n_new, a.shape[-1])
        return jnp.pad(a, ((0, 0), (0, rows - n_new), (0, 0)), mode=mode)

    oa = _attn_sample(padded(qa, Q_ROWS), padded(ka, Q_ROWS), padded(va, Q_ROWS), cache_k, cache_v, n_new)
    c = GLA_CHUNK
    ob, s_new = _gla(padded(qkb, c), padded(lb, c, "edge"), padded(vb, c), padded(rb, c), w["gla_g"], state, F32)
    y = _finish(x2d, oa[:, :n_new].reshape(bn * n_new, A_WIDTH), ob[:, :n_new].reshape(bn * n_new, B_WIDTH), w)
    win_k = kwin.reshape(bn, n_new, A_HEADS, HEAD_DIM)
    win_v = vwin.reshape(bn, n_new, A_HEADS, HEAD_DIM)
    return y.reshape(bn, n_new, d), win_k, win_v, s_new


def kernel(x_prompt, x_sample, cache_win_k, cache_win_v, state_gla, attn_norm_g, w_in, q_norm_g, k_norm_g,
           w_gate2, b_gate, gla_norm_g, w_out, mlp_norm_g, w_up, w_down):
    depth = w_in.shape[0]
    xp, xs = x_prompt, x_sample
    pk, pv, ps, sk, sv, ss = [], [], [], [], [], []
    for l in range(depth):
        w = _layer_weights(attn_norm_g[l], w_in[l], q_norm_g[l], k_norm_g[l], w_gate2[l], b_gate[l],
                           gla_norm_g[l], w_out[l], mlp_norm_g[l], w_up[l], w_down[l])
        xp, k_l, v_l, s_l = _prompt_layer(xp, w)
        pk.append(k_l)
        pv.append(v_l)
        ps.append(s_l)
        xs, k_l, v_l, s_l = _sample_layer(xs, cache_win_k[l], cache_win_v[l], state_gla[l], w)
        sk.append(k_l)
        sv.append(v_l)
        ss.append(s_l)
    return (xp, xs, jnp.stack(pk), jnp.stack(pv), jnp.stack(ps), jnp.stack(sk), jnp.stack(sv), jnp.stack(ss))
```

```python
import functools

import jax
import jax.numpy as jnp
from jax import lax
from jax.experimental import pallas as pl
from jax.experimental.pallas import tpu as pltpu

F32 = jnp.float32
BF16 = jnp.bfloat16

HEAD_DIM = 128
A_HEADS = 8
A_WIDTH = A_HEADS * HEAD_DIM
DILATIONS = (1, 4, 16)
KEYS_BACK = 128
A_BUF = KEYS_BACK * max(DILATIONS)
B_HEADS = 4
B_DK = 128
B_DV = 256
B_WIDTH = B_HEADS * B_DV
GATE_RANK = 16
GATE_TAU = 16.0
GLA_CHUNK = 64
RMS_EPS = 1e-6
LANES = 128
NEG_INF = float("-inf")

VMEM_LIMIT_BYTES = 56 * 1024 * 1024


def _params(semantics):
    return pltpu.CompilerParams(dimension_semantics=semantics, vmem_limit_bytes=VMEM_LIMIT_BYTES)


def _dot(a, b):
    return jnp.dot(a, b, preferred_element_type=F32)


def _dot_nt(a, b):
    return lax.dot_general(a, b, (((1,), (1,)), ((), ())), preferred_element_type=F32)


def _dot_tn(a, b):
    return lax.dot_general(a, b, (((0,), (0,)), ((), ())), preferred_element_type=F32)


def _rms_scale(x, g):
    return x * lax.rsqrt(jnp.mean(x * x, axis=-1, keepdims=True) + RMS_EPS) * g


N_MAIN_TILES = 6


CUMSUM_ROWS = 128


def _inproj_kernel(x_ref, g_ref, w_ref, wg2_ref, bg_ref, qg_ref, kg_ref,
                   qa_ref, ka_ref, va_ref, kwin_ref, vwin_ref, qkb_ref, vb_ref, rb_ref, lb_ref, hn_ref,
                   *, chunk):
    tm = x_ref.shape[0]
    hn_ref[...] = _rms_scale(x_ref[...], g_ref[...]).astype(BF16)

    def proj(j):
        return _dot(hn_ref[...], w_ref[:, j * A_WIDTH:(j + 1) * A_WIDTH])

    def per_head(y, fn, out_ref, win_ref):
        for h in range(A_HEADS):
            sl = slice(h * HEAD_DIM, (h + 1) * HEAD_DIM)
            yh = fn(y[:, sl])
            out_ref[:, sl] = yh
            if win_ref is not None:
                win_ref[pl.ds(h, tm, stride=A_HEADS), :] = yh

    n_main = N_MAIN_TILES * A_WIDTH
    ga = _dot(hn_ref[...], w_ref[:, n_main:])
    z = _dot(ga.astype(BF16), wg2_ref[...]) + bg_ref[...]
    log_a = (jnp.minimum(z, 0.0) - jnp.log1p(jnp.exp(-jnp.abs(z)))) * (1.0 / GATE_TAU)

    per_head(proj(0), lambda y: _rms_scale(y, qg_ref[...]), qa_ref, None)

    shift = chunk.bit_length() - 1
    ti = lax.broadcasted_iota(jnp.int32, (CUMSUM_ROWS, CUMSUM_ROWS), 0)
    tj = lax.broadcasted_iota(jnp.int32, (CUMSUM_ROWS, CUMSUM_ROWS), 1)
    tri = (((ti >> shift) == (tj >> shift)) & (tj <= ti)).astype(F32)
    for r in range(tm // CUMSUM_ROWS):
        rows = slice(r * CUMSUM_ROWS, (r + 1) * CUMSUM_ROWS)
        lb_ref[rows, :] = jnp.dot(tri, log_a[rows, :], preferred_element_type=F32,
                                  precision=lax.Precision.HIGHEST)

    per_head(proj(1), lambda y: _rms_scale(y, kg_ref[...]), ka_ref, kwin_ref)
    per_head(proj(2), lambda y: y, va_ref, vwin_ref)

    y = proj(3)
    half = B_HEADS * B_DK
    qkb_ref[:, :half] = y[:, :half] * (B_DK ** -0.5)
    qkb_ref[:, half:] = y[:, half:]

    vb_ref[...] = proj(4).astype(BF16)
    rb_ref[...] = proj(5)


def _inproj(x, g, w_all, w_g2, b_gate, q_g, k_g, tm, chunk):
    t, d = x.shape
    assert chunk & (chunk - 1) == 0 and CUMSUM_ROWS % chunk == 0 and tm % CUMSUM_ROWS == 0
    assert w_all.shape == (d, N_MAIN_TILES * A_WIDTH + LANES)
    row = lambda i: (i, 0)
    const = lambda i: (0, 0)
    resident = dict(pipeline_mode=pl.Buffered(1))
    out_shapes = (
        jax.ShapeDtypeStruct((t, A_WIDTH), F32),
        jax.ShapeDtypeStruct((t, A_WIDTH), F32),
        jax.ShapeDtypeStruct((t, A_WIDTH), F32),
        jax.ShapeDtypeStruct((t * A_HEADS, HEAD_DIM), F32),
        jax.ShapeDtypeStruct((t * A_HEADS, HEAD_DIM), F32),
        jax.ShapeDtypeStruct((t, 2 * B_HEADS * B_DK), F32),
        jax.ShapeDtypeStruct((t, B_WIDTH), BF16),
        jax.ShapeDtypeStruct((t, B_WIDTH), F32),
        jax.ShapeDtypeStruct((t, B_HEADS * B_DK), F32),
    )
    return pl.pallas_call(
        functools.partial(_inproj_kernel, chunk=chunk),
        grid=(t // tm,),
        in_specs=[
            pl.BlockSpec((tm, d), row),
            pl.BlockSpec((1, d), const),
            pl.BlockSpec(w_all.shape, const, **resident),
            pl.BlockSpec((LANES, B_HEADS * B_DK), const),
            pl.BlockSpec((1, B_HEADS * B_DK), const),
            pl.BlockSpec((1, HEAD_DIM), const),
            pl.BlockSpec((1, HEAD_DIM), const),
        ],
        out_specs=tuple(pl.BlockSpec((tm * s.shape[0] // t, s.shape[1]), row) for s in out_shapes),
        out_shape=out_shapes,
        scratch_shapes=[pltpu.VMEM((tm, d), BF16)],
        compiler_params=_params(("parallel",)),
        name="inproj",
    )(x, g, w_all, w_g2, b_gate, q_g, k_g)


BLK = KEYS_BACK


def _attend_blocks(blocks):
    scores = []
    for q_ref, k_ref, _, start, back in blocks:
        q = q_ref[pl.ds(start, BLK), :]
        k = k_ref[pl.ds(start - back, BLK + back), :]
        s = _dot_nt(q, k) * (HEAD_DIM ** -0.5)
        qi = lax.broadcasted_iota(jnp.int32, s.shape, 0)
        kj = lax.broadcasted_iota(jnp.int32, s.shape, 1)
        valid = (kj >= qi) & (kj <= qi + BLK) if back else kj <= qi
        scores.append(jnp.where(valid, s, NEG_INF))
    probs = []
    for s in scores:
        m = jnp.max(s, axis=-1, keepdims=True)
        p = jnp.exp(s - m)
        probs.append((p.astype(BF16), m, jnp.sum(p, axis=-1, keepdims=True)))
    outs = []
    for (p, m, l), (_, _, v_ref, start, back) in zip(probs, blocks):
        v = v_ref[pl.ds(start - back, BLK + back), :]
        outs.append((_dot(p, v) / l, m + jnp.log(l)))
    return outs


def _attn_prompt_kernel(q_ref, k_ref, v_ref, o_ref,
                        q1_ref, k1_ref, v1_ref, q4_ref, k4_ref, v4_ref, q16_ref, k16_ref, v16_ref,
                        q4f_ref, k4f_ref, v4f_ref,
                        o4_ref, l4_ref, o16_ref, l16_ref, o16x_ref, l16x_ref,
                        *, seq):
    def store(o_out, l_out, pos, stride, o, lse):
        if stride == 1:
            idx = pl.ds(pl.multiple_of(pos, BLK), BLK)
        else:
            idx = pl.ds(pos, BLK, stride=stride)
        o_out[idx, :] = o
        l_out[idx, :] = jnp.broadcast_to(lse, (BLK, LANES))

    sub4, sub16 = seq // 4, seq // 16
    for x_ref, x1_ref, x4_ref, x4f_ref, x16_ref in ((q_ref, q1_ref, q4_ref, q4f_ref, q16_ref),
                                                   (k_ref, k1_ref, k4_ref, k4f_ref, k16_ref),
                                                   (v_ref, v1_ref, v4_ref, v4f_ref, v16_ref)):
        x1_ref[...] = x_ref[...].astype(BF16)
        for r in range(4):
            x4 = x_ref[pl.ds(r, sub4, stride=4), :]
            x4f_ref[r * sub4:(r + 1) * sub4, :] = x4
            x4_ref[r * sub4:(r + 1) * sub4, :] = x4.astype(BF16)
        for r in range(16):
            x16 = x4f_ref[pl.ds((r % 4) * sub4 + r // 4, sub16, stride=4), :]
            x16_ref[r * sub16:(r + 1) * sub16, :] = x16.astype(BF16)

    d1 = (q1_ref, k1_ref, v1_ref)
    d4 = (q4_ref, k4_ref, v4_ref)
    d16 = (q16_ref, k16_ref, v16_ref)

    def run(blocks, dests):
        for (o, lse), (o_out, l_out, pos, stride) in zip(_attend_blocks(blocks), dests):
            store(o_out, l_out, pos, stride, o, lse)

    run([(*d4, r * sub4, 0) for r in range(4)], [(o4_ref, l4_ref, r, 4) for r in range(4)])

    run([(*d16, r * sub16, 0) for r in range(16)],
        [(o16x_ref, l16x_ref, (r % 4) * sub4 + r // 4, 4) for r in range(16)])
    for r in range(4):
        o16_ref[pl.ds(r, sub4, stride=4), :] = o16x_ref[r * sub4:(r + 1) * sub4, :]
        l16_ref[pl.ds(r, sub4, stride=4), :] = l16x_ref[r * sub4:(r + 1) * sub4, :]

    rn = [(r, n) for n in range(1, sub4 // BLK) for r in range(4)]
    run([(*d4, r * sub4 + n * BLK, BLK) for r, n in rn],
        [(o4_ref, l4_ref, r + n * BLK * 4, 4) for r, n in rn])

    starts = [n * BLK for n in range(seq // BLK)]
    half = len(starts) // 2
    for group in (starts[:half], starts[half:]):
        for pos, (o1, lse1) in zip(group, _attend_blocks([(*d1, pos, BLK if pos else 0) for pos in group])):
            sl = pl.ds(pos, BLK)
            l1, l4, l16 = jnp.broadcast_to(lse1, (BLK, LANES)), l4_ref[sl, :], l16_ref[sl, :]
            m = jnp.maximum(jnp.maximum(l1, l4), l16)
            e1, e4, e16 = jnp.exp(l1 - m), jnp.exp(l4 - m), jnp.exp(l16 - m)
            num = e1 * o1 + e4 * o4_ref[sl, :] + e16 * o16_ref[sl, :]
            o_ref[sl, :] = (num / (e1 + e4 + e16)).astype(o_ref.dtype)


def _attn_prompt(qa, ka, va):
    bn, seq, _ = qa.shape
    assert seq == A_BUF, "prompt attention is written for a sequence of exactly the largest window"
    spec = pl.BlockSpec((None, seq, HEAD_DIM), lambda b, h: (b, 0, h))
    return pl.pallas_call(
        functools.partial(_attn_prompt_kernel, seq=seq),
        grid=(bn, A_HEADS),
        in_specs=[spec, spec, spec],
        out_specs=spec,
        out_shape=jax.ShapeDtypeStruct((bn, seq, A_WIDTH), BF16),
        scratch_shapes=[pltpu.VMEM((seq, HEAD_DIM), BF16)] * 9 + [pltpu.VMEM((seq, LANES), F32)] * 9,
        compiler_params=_params(("parallel", "parallel")),
        name="attn_prompt",
    )(qa, ka, va)


Q_ROWS = 8
NEAR = 4 * KEYS_BACK


def _attn_sample_kernel(q_ref, kn_ref, vn_ref, nk_ref, nv_ref, fk_ref, fv_ref, o_ref, *, n_new):
    scale = HEAD_DIM ** -0.5
    ext = NEAR + LANES
    total = ext + n_new * KEYS_BACK

    t = lax.broadcasted_iota(jnp.int32, (Q_ROWS, total), 0) & 3
    c = lax.broadcasted_iota(jnp.int32, (Q_ROWS, total), 1)
    c1 = c - (NEAR - KEYS_BACK)
    in1 = (c1 >= t) & (c1 <= KEYS_BACK + t)
    in4 = ((c & 3) == t) & (c <= NEAR + t)
    in16 = ((c - ext) >> 7) == t
    mult = in1.astype(F32) + in4.astype(F32) + in16.astype(F32)

    def rows_of(near_ref, far_ref, new, h):
        near = near_ref[pl.ds(h, NEAR, stride=A_HEADS), :]
        far = [far_ref[:, tt * A_HEADS + h, :] for tt in range(n_new)]
        pad = jnp.zeros((LANES - Q_ROWS, HEAD_DIM), F32)
        return jnp.concatenate([near, new, pad] + far, axis=0).astype(BF16)

    heads = range(A_HEADS)
    sl = lambda h: slice(h * HEAD_DIM, (h + 1) * HEAD_DIM)
    qs = [q_ref[:, sl(h)].astype(BF16) for h in heads]
    kns = [kn_ref[:, sl(h)] for h in heads]
    vns = [vn_ref[:, sl(h)] for h in heads]
    scores = [_dot_nt(qs[h], rows_of(nk_ref, fk_ref, kns[h], h)) * scale for h in heads]
    probs = []
    for h in heads:
        s_self = jnp.sum(qs[h].astype(F32) * kns[h].astype(BF16).astype(F32), axis=-1, keepdims=True) * scale
        s = jnp.where(mult > 0.0, scores[h], NEG_INF)
        m = jnp.maximum(jnp.max(s, axis=-1, keepdims=True), s_self)
        p = jnp.exp(s - m) * mult
        p_self = jnp.exp(s_self - m)
        probs.append((p.astype(BF16), p_self, jnp.sum(p, axis=-1, keepdims=True) + p_self))
    for h in heads:
        p, p_self, l = probs[h]
        acc = _dot(p, rows_of(nv_ref, fv_ref, vns[h], h))
        acc = acc + p_self.astype(BF16).astype(F32) * vns[h].astype(BF16).astype(F32)
        o_ref[:, sl(h)] = acc / l


def _attn_sample(qn, kn, vn, cache_k, cache_v, n_new):
    bn, cache_len = cache_k.shape[:2]
    assert KEYS_BACK == 128 and cache_len == A_BUF and n_new == 4, "written for a full window cache and 4 new tokens"
    new_spec = pl.BlockSpec((None, Q_ROWS, A_WIDTH), lambda b: (b, 0, 0))
    near_spec = pl.BlockSpec((None, NEAR * A_HEADS, HEAD_DIM), lambda b: (b, cache_len // NEAR - 1, 0))
    far_spec = pl.BlockSpec((None, KEYS_BACK, n_new * A_HEADS, HEAD_DIM), lambda b: (b, 0, 0, 0))
    near = lambda c: c.reshape(bn, cache_len * A_HEADS, HEAD_DIM)
    far = lambda c: c.reshape(bn, KEYS_BACK, 16 * A_HEADS, HEAD_DIM)
    return pl.pallas_call(
        functools.partial(_attn_sample_kernel, n_new=n_new),
        grid=(bn,),
        in_specs=[new_spec, new_spec, new_spec, near_spec, near_spec, far_spec, far_spec],
        out_specs=new_spec,
        out_shape=jax.ShapeDtypeStruct((bn, Q_ROWS, A_WIDTH), F32),
        compiler_params=_params(("parallel",)),
        name="attn_sample",
    )(qn, kn, vn, near(cache_k), near(cache_v), far(cache_k), far(cache_v))


def _gla_kernel(q_ref, k_ref, lb_ref, v_ref, rb_ref, g_ref, s0_ref, o_ref, sfin_ref, st_ref,
                *, n_chunks, group, heads):
    c = GLA_CHUNK
    for hd in range(heads):
        st_ref[hd] = s0_ref[hd].T
    ti = lax.broadcasted_iota(jnp.int32, (c, c), 0)
    tj = lax.broadcasted_iota(jnp.int32, (c, c), 1)
    causal = tj <= ti

    def group_body(gi, carry):
        rows = [pl.ds(pl.multiple_of((gi * group + u) * c, c), c) for u in range(group)]
        items = [(hd, r) for hd in range(heads) for r in rows]
        dk = lambda hd: slice(hd * B_DK, (hd + 1) * B_DK)
        dv = lambda hd: slice(hd * B_DV, (hd + 1) * B_DV)
        q_dec, k_inv, k_end, decay = [], [], [], []
        for hd, r in items:
            b = lb_ref[r, dk(hd)]
            b_last = b[c - 1:c, :]
            q = q_ref[r, dk(hd)]
            k = k_ref[r, dk(hd)]
            q_dec.append((q * jnp.exp(b)).astype(BF16))
            k_inv.append((k * jnp.exp(-b)).astype(BF16))
            k_end.append((k * jnp.exp(b_last - b)).astype(BF16))
            decay.append(jnp.exp(b_last))
        att = [jnp.where(causal, _dot_nt(qd, ki), 0.0).astype(BF16) for qd, ki in zip(q_dec, k_inv)]
        delta = [_dot_tn(v_ref[r, dv(hd)], ke) for (hd, r), ke in zip(items, k_end)]
        before = []
        for hd in range(heads):
            st = st_ref[hd]
            for i in range(hd * group, (hd + 1) * group):
                before.append(st.astype(BF16))
                st = st * decay[i] + delta[i]
            st_ref[hd] = st
        for (hd, r), a, qd, sb in zip(items, att, q_dec, before):
            o = _dot(a, v_ref[r, dv(hd)]) + _dot_nt(qd, sb)
            gated = _rms_scale(o, g_ref[hd]) * jax.nn.silu(rb_ref[r, dv(hd)])
            o_ref[r, dv(hd)] = gated.astype(o_ref.dtype)
        return carry

    lax.fori_loop(0, n_chunks // group, group_body, 0)
    for hd in range(heads):
        sfin_ref[hd] = st_ref[hd].T


def _gla(qkb, lb, vb, rb, g, s0, out_dtype):
    bn, t, _ = qkb.shape
    assert t % GLA_CHUNK == 0
    n_chunks = t // GLA_CHUNK
    heads, group = (2, 8) if n_chunks % 8 == 0 else (B_HEADS, 1)
    nhb = B_HEADS // heads
    return pl.pallas_call(
        functools.partial(_gla_kernel, n_chunks=n_chunks, group=group, heads=heads),
        grid=(bn, nhb),
        in_specs=[
            pl.BlockSpec((None, t, heads * B_DK), lambda b, h: (b, 0, h)),
            pl.BlockSpec((None, t, heads * B_DK), lambda b, h: (b, 0, nhb + h)),
            pl.BlockSpec((None, t, heads * B_DK), lambda b, h: (b, 0, h)),
            pl.BlockSpec((None, t, heads * B_DV), lambda b, h: (b, 0, h)),
            pl.BlockSpec((None, t, heads * B_DV), lambda b, h: (b, 0, h)),
            pl.BlockSpec((heads, 1, B_DV), lambda b, h: (h, 0, 0)),
            pl.BlockSpec((None, heads, B_DK, B_DV), lambda b, h: (b, h, 0, 0)),
        ],
        out_specs=(
            pl.BlockSpec((None, t, heads * B_DV), lambda b, h: (b, 0, h)),
            pl.BlockSpec((None, heads, B_DK, B_DV), lambda b, h: (b, h, 0, 0)),
        ),
        out_shape=(
            jax.ShapeDtypeStruct((bn, t, B_WIDTH), out_dtype),
            jax.ShapeDtypeStruct((bn, B_HEADS, B_DK, B_DV), F32),
        ),
        scratch_shapes=[pltpu.VMEM((heads, B_DV, B_DK), F32)],
        compiler_params=_params(("parallel", "parallel")),
        name="gla",
    )(qkb, qkb, lb, vb, rb, g, s0)


def _outproj_kernel(x_ref, ma_ref, mb_ref, wa_ref, wb_ref, g_ref, x1_ref, hm_ref):
    x1 = x_ref[...] + _dot(ma_ref[...].astype(BF16), wa_ref[...]) + _dot(mb_ref[...].astype(BF16), wb_ref[...])
    x1_ref[...] = x1
    hm_ref[...] = _rms_scale(x1, g_ref[...]).astype(BF16)


def _outproj(x, mix_a, mix_b, w_a, w_b, g, tm):
    t, d = x.shape
    row = lambda i: (i, 0)
    const = lambda i: (0, 0)
    return pl.pallas_call(
        _outproj_kernel,
        grid=(t // tm,),
        in_specs=[
            pl.BlockSpec((tm, d), row),
            pl.BlockSpec((tm, A_WIDTH), row),
            pl.BlockSpec((tm, B_WIDTH), row),
            pl.BlockSpec((A_WIDTH, d), const),
            pl.BlockSpec((B_WIDTH, d), const),
            pl.BlockSpec((1, d), const),
        ],
        out_specs=(pl.BlockSpec((tm, d), row), pl.BlockSpec((tm, d), row)),
        out_shape=(jax.ShapeDtypeStruct((t, d), F32), jax.ShapeDtypeStruct((t, d), BF16)),
        compiler_params=_params(("parallel",)),
        name="outproj",
    )(x, mix_a, mix_b, w_a, w_b, g)


def _mlp_kernel(x1_ref, hm_ref, wu_ref, wd_ref, y_ref):
    f = pl.program_id(1)

    @pl.when(f == 0)
    def _():
        y_ref[...] = x1_ref[...]

    u = jnp.square(jnp.maximum(_dot(hm_ref[...], wu_ref[...]), 0.0))
    y_ref[...] += _dot(u.astype(BF16), wd_ref[...])


def _mlp(x1, hm, w_up, w_down, tm, tf):
    t, d = x1.shape
    d_ff = w_up.shape[1]
    row = lambda i, f: (i, 0)
    return pl.pallas_call(
        _mlp_kernel,
        grid=(t // tm, d_ff // tf),
        in_specs=[
            pl.BlockSpec((tm, d), row),
            pl.BlockSpec((tm, d), row),
            pl.BlockSpec((d, tf), lambda i, f: (0, f)),
            pl.BlockSpec((tf, d), lambda i, f: (f, 0)),
        ],
        out_specs=pl.BlockSpec((tm, d), row),
        out_shape=jax.ShapeDtypeStruct((t, d), F32),
        compiler_params=_params(("parallel", "arbitrary")),
        name="mlp",
    )(x1, hm, w_up, w_down)


def _row_tile(t):
    return 512 if t % 512 == 0 else t


def _layer_weights(attn_norm_g, w_in, q_norm_g, k_norm_g, w_gate2, b_gate, gla_norm_g, w_out, mlp_norm_g,
                   w_up, w_down):
    w_g2 = jnp.pad(w_gate2, ((0, LANES - GATE_RANK), (0, 0)))
    return dict(
        attn_g=attn_norm_g[None, :],
        w_all=jnp.pad(w_in.astype(BF16), ((0, 0), (0, LANES - GATE_RANK))),
        w_g2=w_g2.astype(BF16),
        b_gate=b_gate[None, :],
        q_g=q_norm_g[None, :],
        k_g=k_norm_g[None, :],
        gla_g=gla_norm_g[:, None, :],
        w_out_a=w_out[:A_WIDTH].astype(BF16),
        w_out_b=w_out[A_WIDTH:].astype(BF16),
        mlp_g=mlp_norm_g[None, :],
        w_up=w_up.astype(BF16),
        w_down=w_down.astype(BF16),
    )


def _mixer_inputs(x2d, w, chunk):
    t = x2d.shape[0]
    tm = 256 if t % 256 == 0 else t
    return _inproj(x2d, w["attn_g"], w["w_all"], w["w_g2"], w["b_gate"], w["q_g"], w["k_g"], tm, chunk)


def _finish(x2d, mix_a, mix_b, w):
    t = x2d.shape[0]
    x1, hm = _outproj(x2d, mix_a, mix_b, w["w_out_a"], w["w_out_b"], w["mlp_g"], _row_tile(t))
    return _mlp(x1, hm, w["w_up"], w["w_down"], _row_tile(t), 1024)


def _prompt_layer(xp, w):
    bn, seq, d = xp.shape
    x2d = xp.reshape(bn * seq, d)
    assert seq % GLA_CHUNK == 0, "GLA chunks must not straddle batch rows"
    qa, ka, va, kwin, vwin, qkb, vb, rb, lb = _mixer_inputs(x2d, w, GLA_CHUNK)
    b3 = lambda a: a.reshape(bn, seq, a.shape[-1])
    oa = _attn_prompt(b3(qa), b3(ka), b3(va))
    s0 = jnp.zeros((bn, B_HEADS, B_DK, B_DV), F32)
    ob, s_fin = _gla(b3(qkb), b3(lb), b3(vb), b3(rb), w["gla_g"], s0, BF16)
    y = _finish(x2d, oa.reshape(bn * seq, A_WIDTH), ob.reshape(bn * seq, B_WIDTH), w)
    lp = min(A_BUF, seq)
    win_k = kwin.reshape(bn, seq, A_HEADS, HEAD_DIM)[:, seq - lp:]
    win_v = vwin.reshape(bn, seq, A_HEADS, HEAD_DIM)[:, seq - lp:]
    return y.reshape(bn, seq, d), win_k, win_v, s_fin


def _sample_layer(xs, cache_k, cache_v, state, w):
    bn, n_new, d = xs.shape
    x2d = xs.reshape(bn * n_new, d)
    assert n_new <= GLA_CHUNK, "the new tokens of one batch row form a single GLA chunk"
    qa, ka, va, kwin, vwin, qkb, vb, rb, lb = _mixer_inputs(x2d, w, n_new)

    def padded(a, rows, mode="constant"):
        a = a.reshape(bn, n_new, a.shape[-1])
        return jnp.pad(a, ((0, 0), (0, rows - n_new), (0, 0)), mode=mode)

    oa = _attn_sample(padded(qa, Q_ROWS), padded(ka, Q_ROWS), padded(va, Q_ROWS), cache_k, cache_v, n_new)
    c = GLA_CHUNK
    ob, s_new = _gla(padded(qkb, c), padded(lb, c, "edge"), padded(vb, c), padded(rb, c), w["gla_g"], state, F32)
    y = _finish(x2d, oa[:, :n_new].reshape(bn * n_new, A_WIDTH), ob[:, :n_new].reshape(bn * n_new, B_WIDTH), w)
    win_k = kwin.reshape(bn, n_new, A_HEADS, HEAD_DIM)
    win_v = vwin.reshape(bn, n_new, A_HEADS, HEAD_DIM)
    return y.reshape(bn, n_new, d), win_k, win_v, s_new


def kernel(x_prompt, x_sample, cache_win_k, cache_win_v, state_gla, attn_norm_g, w_in, q_norm_g, k_norm_g,
           w_gate2, b_gate, gla_norm_g, w_out, mlp_norm_g, w_up, w_down):
    depth = w_in.shape[0]
    xp, xs = x_prompt, x_sample
    pk, pv, ps, sk, sv, ss = [], [], [], [], [], []
    for l in range(depth):
        w = _layer_weights(attn_norm_g[l], w_in[l], q_norm_g[l], k_norm_g[l], w_gate2[l], b_gate[l],
                           gla_norm_g[l], w_out[l], mlp_norm_g[l], w_up[l], w_down[l])
        xp, k_l, v_l, s_l = _prompt_layer(xp, w)
        pk.append(k_l)
        pv.append(v_l)
        ps.append(s_l)
        xs, k_l, v_l, s_l = _sample_layer(xs, cache_win_k[l], cache_win_v[l], state_gla[l], w)
        sk.append(k_l)
        sv.append(v_l)
        ss.append(s_l)
    return (xp, xs, jnp.stack(pk), jnp.stack(pv), jnp.stack(ps), jnp.stack(sk), jnp.stack(sv), jnp.stack(ss))
```

```python
import functools

import jax
import jax.numpy as jnp
from jax import lax
from jax.experimental import pallas as pl
from jax.experimental.pallas import tpu as pltpu

F32 = jnp.float32
BF16 = jnp.bfloat16

HEAD_DIM = 128
A_HEADS = 8
A_WIDTH = A_HEADS * HEAD_DIM
DILATIONS = (1, 4, 16)
KEYS_BACK = 128
A_BUF = KEYS_BACK * max(DILATIONS)
B_HEADS = 4
B_DK = 128
B_DV = 256
B_WIDTH = B_HEADS * B_DV
GATE_RANK = 16
GATE_TAU = 16.0
GLA_CHUNK = 64
RMS_EPS = 1e-6
LANES = 128
NEG_INF = float("-inf")

VMEM_LIMIT_BYTES = 56 * 1024 * 1024


def _params(semantics):
    return pltpu.CompilerParams(dimension_semantics=semantics, vmem_limit_bytes=VMEM_LIMIT_BYTES)


def _dot(a, b):
    return jnp.dot(a, b, preferred_element_type=F32)


def _dot_nt(a, b):
    return lax.dot_general(a, b, (((1,), (1,)), ((), ())), preferred_element_type=F32)


def _dot_tn(a, b):
    return lax.dot_general(a, b, (((0,), (0,)), ((), ())), preferred_element_type=F32)


def _rms_scale(x, g):
    return x * lax.rsqrt(jnp.mean(x * x, axis=-1, keepdims=True) + RMS_EPS) * g


N_MAIN_TILES = 6


CUMSUM_ROWS = 128


def _inproj_kernel(x_ref, g_ref, w_ref, wg2_ref, bg_ref, qg_ref, kg_ref,
                   qa_ref, ka_ref, va_ref, kwin_ref, vwin_ref, qkb_ref, vb_ref, rb_ref, lb_ref, hn_ref,
                   *, chunk):
    tm = x_ref.shape[0]
    hn_ref[...] = _rms_scale(x_ref[...], g_ref[...]).astype(BF16)

    def proj(j):
        return _dot(hn_ref[...], w_ref[:, j * A_WIDTH:(j + 1) * A_WIDTH])

    def per_head(y, fn, out_ref, win_ref):
        for h in range(A_HEADS):
            sl = slice(h * HEAD_DIM, (h + 1) * HEAD_DIM)
            yh = fn(y[:, sl])
            out_ref[:, sl] = yh
            if win_ref is not None:
                win_ref[pl.ds(h, tm, stride=A_HEADS), :] = yh

    n_main = N_MAIN_TILES * A_WIDTH
    ga = _dot(hn_ref[...], w_ref[:, n_main:])
    z = _dot(ga.astype(BF16), wg2_ref[...]) + bg_ref[...]
    log_a = (jnp.minimum(z, 0.0) - jnp.log1p(jnp.exp(-jnp.abs(z)))) * (1.0 / GATE_TAU)

    per_head(proj(0), lambda y: _rms_scale(y, qg_ref[...]), qa_ref, None)

    shift = chunk.bit_length() - 1
    ti = lax.broadcasted_iota(jnp.int32, (CUMSUM_ROWS, CUMSUM_ROWS), 0)
    tj = lax.broadcasted_iota(jnp.int32, (CUMSUM_ROWS, CUMSUM_ROWS), 1)
    tri = (((ti >> shift) == (tj >> shift)) & (tj <= ti)).astype(F32)
    for r in range(tm // CUMSUM_ROWS):
        rows = slice(r * CUMSUM_ROWS, (r + 1) * CUMSUM_ROWS)
        lb_ref[rows, :] = jnp.dot(tri, log_a[rows, :], preferred_element_type=F32,
                                  precision=lax.Precision.HIGHEST)

    per_head(proj(1), lambda y: _rms_scale(y, kg_ref[...]), ka_ref, kwin_ref)
    per_head(proj(2), lambda y: y, va_ref, vwin_ref)

    y = proj(3)
    half = B_HEADS * B_DK
    qkb_ref[:, :half] = y[:, :half] * (B_DK ** -0.5)
    qkb_ref[:, half:] = y[:, half:]

    vb_ref[...] = proj(4).astype(BF16)
    rb_ref[...] = proj(5)


def _inproj(x, g, w_all, w_g2, b_gate, q_g, k_g, tm, chunk):
    t, d = x.shape
    assert chunk & (chunk - 1) == 0 and CUMSUM_ROWS % chunk == 0 and tm % CUMSUM_ROWS == 0
    assert w_all.shape == (d, N_MAIN_TILES * A_WIDTH + LANES)
    row = lambda i: (i, 0)
    const = lambda i: (0, 0)
    resident = dict(pipeline_mode=pl.Buffered(1))
    out_shapes = (
        jax.ShapeDtypeStruct((t, A_WIDTH), F32),
        jax.ShapeDtypeStruct((t, A_WIDTH), F32),
        jax.ShapeDtypeStruct((t, A_WIDTH), F32),
        jax.ShapeDtypeStruct((t * A_HEADS, HEAD_DIM), F32),
        jax.ShapeDtypeStruct((t * A_HEADS, HEAD_DIM), F32),
        jax.ShapeDtypeStruct((t, 2 * B_HEADS * B_DK), F32),
        jax.ShapeDtypeStruct((t, B_WIDTH), BF16),
        jax.ShapeDtypeStruct((t, B_WIDTH), F32),
        jax.ShapeDtypeStruct((t, B_HEADS * B_DK), F32),
    )
    return pl.pallas_call(
        functools.partial(_inproj_kernel, chunk=chunk),
        grid=(t // tm,),
        in_specs=[
            pl.BlockSpec((tm, d), row),
            pl.BlockSpec((1, d), const),
            pl.BlockSpec(w_all.shape, const, **resident),
            pl.BlockSpec((LANES, B_HEADS * B_DK), const),
            pl.BlockSpec((1, B_HEADS * B_DK), const),
            pl.BlockSpec((1, HEAD_DIM), const),
            pl.BlockSpec((1, HEAD_DIM), const),
        ],
        out_specs=tuple(pl.BlockSpec((tm * s.shape[0] // t, s.shape[1]), row) for s in out_shapes),
        out_shape=out_shapes,
        scratch_shapes=[pltpu.VMEM((tm, d), BF16)],
        compiler_params=_params(("parallel",)),
        name="inproj",
    )(x, g, w_all, w_g2, b_gate, q_g, k_g)


BLK = KEYS_BACK


def _attend_blocks(blocks, first_bias_ref, band_bias_ref):
    scores = []
    for q_ref, k_ref, _, start, back in blocks:
        q = q_ref[pl.ds(start, BLK), :]
        k = k_ref[pl.ds(start - back, BLK + back), :]
        bias = band_bias_ref[...] if back else first_bias_ref[...]
        scores.append(_dot_nt(q, k) * (HEAD_DIM ** -0.5) + bias)
    probs = []
    for s in scores:
        m = jnp.max(s, axis=-1, keepdims=True)
        p = jnp.exp(s - m)
        probs.append((p.astype(BF16), m, jnp.sum(p, axis=-1, keepdims=True)))
    outs = []
    for (p, m, l), (_, _, v_ref, start, back) in zip(probs, blocks):
        v = v_ref[pl.ds(start - back, BLK + back), :]
        outs.append((_dot(p, v) / l, m + jnp.log(l)))
    return outs


def _attn_prompt_kernel(q_ref, k_ref, v_ref, o_ref,
                        q1_ref, k1_ref, v1_ref, q4_ref, k4_ref, v4_ref, q16_ref, k16_ref, v16_ref,
                        q4f_ref, k4f_ref, v4f_ref,
                        o1_ref, l1_ref, o4_ref, l4_ref, o16_ref, l16_ref, o16x_ref, l16x_ref,
                        first_bias_ref, band_bias_ref,
                        *, seq):
    qi = lax.broadcasted_iota(jnp.int32, (BLK, BLK), 0)
    kj = lax.broadcasted_iota(jnp.int32, (BLK, BLK), 1)
    first_bias_ref[...] = jnp.where(kj <= qi, 0.0, NEG_INF).astype(F32)
    qi = lax.broadcasted_iota(jnp.int32, (BLK, 2 * BLK), 0)
    kj = lax.broadcasted_iota(jnp.int32, (BLK, 2 * BLK), 1)
    band_bias_ref[...] = jnp.where((kj >= qi) & (kj <= qi + BLK), 0.0, NEG_INF).astype(F32)

    def store(o_out, l_out, pos, stride, o, lse):
        if stride == 1:
            idx = pl.ds(pl.multiple_of(pos, BLK), BLK)
        else:
            idx = pl.ds(pos, BLK, stride=stride)
        o_out[idx, :] = o
        l_out[idx, :] = jnp.broadcast_to(lse, (BLK, LANES))

    sub4, sub16 = seq // 4, seq // 16
    for x_ref, x1_ref, x4_ref, x4f_ref, x16_ref in ((q_ref, q1_ref, q4_ref, q4f_ref, q16_ref),
                                                   (k_ref, k1_ref, k4_ref, k4f_ref, k16_ref),
                                                   (v_ref, v1_ref, v4_ref, v4f_ref, v16_ref)):
        x1_ref[...] = x_ref[...].astype(BF16)
        for r in range(4):
            x4 = x_ref[pl.ds(r, sub4, stride=4), :]
            x4f_ref[r * sub4:(r + 1) * sub4, :] = x4
            x4_ref[r * sub4:(r + 1) * sub4, :] = x4.astype(BF16)
        for r in range(16):
            x16 = x4f_ref[pl.ds((r % 4) * sub4 + r // 4, sub16, stride=4), :]
            x16_ref[r * sub16:(r + 1) * sub16, :] = x16.astype(BF16)

    d1 = (q1_ref, k1_ref, v1_ref)
    d4 = (q4_ref, k4_ref, v4_ref)
    d16 = (q16_ref, k16_ref, v16_ref)

    def run(blocks, dests):
        outs = _attend_blocks(blocks, first_bias_ref, band_bias_ref)
        for (o, lse), (o_out, l_out, pos, stride) in zip(outs, dests):
            store(o_out, l_out, pos, stride, o, lse)

    run([(*d1, 0, 0)] + [(*d4, r * sub4, 0) for r in range(4)],
        [(o1_ref, l1_ref, 0, 1)] + [(o4_ref, l4_ref, r, 4) for r in range(4)])

    run([(*d16, r * sub16, 0) for r in range(16)],
        [(o16x_ref, l16x_ref, (r % 4) * sub4 + r // 4, 4) for r in range(16)])
    for r in range(4):
        o16_ref[pl.ds(r, sub4, stride=4), :] = o16x_ref[r * sub4:(r + 1) * sub4, :]
        l16_ref[pl.ds(r, sub4, stride=4), :] = l16x_ref[r * sub4:(r + 1) * sub4, :]

    group1 = 15

    def d1_body(i, carry):
        ns = [1 + i * group1 + u for u in range(group1)]
        run([(*d1, pl.multiple_of(n * BLK, BLK), BLK) for n in ns],
            [(o1_ref, l1_ref, n * BLK, 1) for n in ns])
        return carry

    lax.fori_loop(0, (seq // BLK - 1) // group1, d1_body, 0)

    rn = [(r, n) for n in range(1, sub4 // BLK) for r in range(4)]
    run([(*d4, r * sub4 + n * BLK, BLK) for r, n in rn],
        [(o4_ref, l4_ref, r + n * BLK * 4, 4) for r, n in rn])

    rows = 256

    def merge_body(c, carry):
        sl = pl.ds(pl.multiple_of(c * rows, rows), rows)
        l1, l4, l16 = l1_ref[sl, :], l4_ref[sl, :], l16_ref[sl, :]
        m = jnp.maximum(jnp.maximum(l1, l4), l16)
        e1, e4, e16 = jnp.exp(l1 - m), jnp.exp(l4 - m), jnp.exp(l16 - m)
        num = e1 * o1_ref[sl, :] + e4 * o4_ref[sl, :] + e16 * o16_ref[sl, :]
        o_ref[sl, :] = (num / (e1 + e4 + e16)).astype(o_ref.dtype)
        return carry

    lax.fori_loop(0, seq // rows, merge_body, 0)


def _attn_prompt(qa, ka, va):
    bn, seq, _ = qa.shape
    assert seq == A_BUF, "prompt attention is written for a sequence of exactly the largest window"
    spec = pl.BlockSpec((None, seq, HEAD_DIM), lambda b, h: (b, 0, h))
    return pl.pallas_call(
        functools.partial(_attn_prompt_kernel, seq=seq),
        grid=(bn, A_HEADS),
        in_specs=[spec, spec, spec],
        out_specs=spec,
        out_shape=jax.ShapeDtypeStruct((bn, seq, A_WIDTH), BF16),
        scratch_shapes=([pltpu.VMEM((seq, HEAD_DIM), BF16)] * 9 + [pltpu.VMEM((seq, LANES), F32)] * 11
                        + [pltpu.VMEM((BLK, BLK), F32), pltpu.VMEM((BLK, 2 * BLK), F32)]),
        compiler_params=_params(("parallel", "parallel")),
        name="attn_prompt",
    )(qa, ka, va)


Q_ROWS = 8
NEAR = 4 * KEYS_BACK


def _attn_sample_kernel(q_ref, kn_ref, vn_ref, nk_ref, nv_ref, fk_ref, fv_ref, o_ref, *, n_new):
    scale = HEAD_DIM ** -0.5
    ext = NEAR + LANES
    total = ext + n_new * KEYS_BACK

    t = lax.broadcasted_iota(jnp.int32, (Q_ROWS, total), 0) & 3
    c = lax.broadcasted_iota(jnp.int32, (Q_ROWS, total), 1)
    c1 = c - (NEAR - KEYS_BACK)
    in1 = (c1 >= t) & (c1 <= KEYS_BACK + t)
    in4 = ((c & 3) == t) & (c <= NEAR + t)
    in16 = ((c - ext) >> 7) == t
    mult = in1.astype(F32) + in4.astype(F32) + in16.astype(F32)

    def rows_of(near_ref, far_ref, new, h):
        near = near_ref[pl.ds(h, NEAR, stride=A_HEADS), :]
        far = [far_ref[:, tt * A_HEADS + h, :] for tt in range(n_new)]
        pad = jnp.zeros((LANES - Q_ROWS, HEAD_DIM), F32)
        return jnp.concatenate([near, new, pad] + far, axis=0).astype(BF16)

    heads = range(A_HEADS)
    sl = lambda h: slice(h * HEAD_DIM, (h + 1) * HEAD_DIM)
    qs = [q_ref[:, sl(h)].astype(BF16) for h in heads]
    kns = [kn_ref[:, sl(h)] for h in heads]
    vns = [vn_ref[:, sl(h)] for h in heads]
    scores = [_dot_nt(qs[h], rows_of(nk_ref, fk_ref, kns[h], h)) * scale for h in heads]
    probs = []
    for h in heads:
        s_self = jnp.sum(qs[h].astype(F32) * kns[h].astype(BF16).astype(F32), axis=-1, keepdims=True) * scale
        s = jnp.where(mult > 0.0, scores[h], NEG_INF)
        m = jnp.maximum(jnp.max(s, axis=-1, keepdims=True), s_self)
        p = jnp.exp(s - m) * mult
        p_self = jnp.exp(s_self - m)
        probs.append((p.astype(BF16), p_self, jnp.sum(p, axis=-1, keepdims=True) + p_self))
    for h in heads:
        p, p_self, l = probs[h]
        acc = _dot(p, rows_of(nv_ref, fv_ref, vns[h], h))
        acc = acc + p_self.astype(BF16).astype(F32) * vns[h].astype(BF16).astype(F32)
        o_ref[:, sl(h)] = acc / l


def _attn_sample(qn, kn, vn, cache_k, cache_v, n_new):
    bn, cache_len = cache_k.shape[:2]
    assert KEYS_BACK == 128 and cache_len == A_BUF and n_new == 4, "written for a full window cache and 4 new tokens"
    new_spec = pl.BlockSpec((None, Q_ROWS, A_WIDTH), lambda b: (b, 0, 0))
    near_spec = pl.BlockSpec((None, NEAR * A_HEADS, HEAD_DIM), lambda b: (b, cache_len // NEAR - 1, 0))
    far_spec = pl.BlockSpec((None, KEYS_BACK, n_new * A_HEADS, HEAD_DIM), lambda b: (b, 0, 0, 0))
    near = lambda c: c.reshape(bn, cache_len * A_HEADS, HEAD_DIM)
    far = lambda c: c.reshape(bn, KEYS_BACK, 16 * A_HEADS, HEAD_DIM)
    return pl.pallas_call(
        functools.partial(_attn_sample_kernel, n_new=n_new),
        grid=(bn,),
        in_specs=[new_spec, new_spec, new_spec, near_spec, near_spec, far_spec, far_spec],
        out_specs=new_spec,
        out_shape=jax.ShapeDtypeStruct((bn, Q_ROWS, A_WIDTH), F32),
        compiler_params=_params(("parallel",)),
        name="attn_sample",
    )(qn, kn, vn, near(cache_k), near(cache_v), far(cache_k), far(cache_v))


def _gla_kernel(q_ref, k_ref, lb_ref, v_ref, rb_ref, g_ref, s0_ref, o_ref, sfin_ref, st_ref,
                *, n_chunks, group, heads):
    c = GLA_CHUNK
    for hd in range(heads):
        st_ref[hd] = s0_ref[hd].T
    ti = lax.broadcasted_iota(jnp.int32, (c, c), 0)
    tj = lax.broadcasted_iota(jnp.int32, (c, c), 1)
    causal = tj <= ti

    def group_body(gi, carry):
        rows = [pl.ds(pl.multiple_of((gi * group + u) * c, c), c) for u in range(group)]
        items = [(hd, r) for hd in range(heads) for r in rows]
        dk = lambda hd: slice(hd * B_DK, (hd + 1) * B_DK)
        dv = lambda hd: slice(hd * B_DV, (hd + 1) * B_DV)
        q_dec, k_inv, k_end, decay = [], [], [], []
        for hd, r in items:
            b = lb_ref[r, dk(hd)]
            b_last = b[c - 1:c, :]
            q = q_ref[r, dk(hd)]
            k = k_ref[r, dk(hd)]
            q_dec.append((q * jnp.exp(b)).astype(BF16))
            k_inv.append((k * jnp.exp(-b)).astype(BF16))
            k_end.append((k * jnp.exp(b_last - b)).astype(BF16))
            decay.append(jnp.exp(b_last))
        att = [jnp.where(causal, _dot_nt(qd, ki), 0.0).astype(BF16) for qd, ki in zip(q_dec, k_inv)]
        delta = [_dot_tn(v_ref[r, dv(hd)], ke) for (hd, r), ke in zip(items, k_end)]
        before = []
        for hd in range(heads):
            st = st_ref[hd]
            for i in range(hd * group, (hd + 1) * group):
                before.append(st.astype(BF16))
                st = st * decay[i] + delta[i]
            st_ref[hd] = st
        for (hd, r), a, qd, sb in zip(items, att, q_dec, before):
            o = _dot(a, v_ref[r, dv(hd)]) + _dot_nt(qd, sb)
            gated = _rms_scale(o, g_ref[hd]) * jax.nn.silu(rb_ref[r, dv(hd)])
            o_ref[r, dv(hd)] = gated.astype(o_ref.dtype)
        return carry

    lax.fori_loop(0, n_chunks // group, group_body, 0)
    for hd in range(heads):
        sfin_ref[hd] = st_ref[hd].T


def _gla(qkb, lb, vb, rb, g, s0, out_dtype):
    bn, t, _ = qkb.shape
    assert t % GLA_CHUNK == 0
    n_chunks = t // GLA_CHUNK
    heads, group = (2, 8) if n_chunks % 8 == 0 else (B_HEADS, 1)
    nhb = B_HEADS // heads
    return pl.pallas_call(
        functools.partial(_gla_kernel, n_chunks=n_chunks, group=group, heads=heads),
        grid=(bn, nhb),
        in_specs=[
            pl.BlockSpec((None, t, heads * B_DK), lambda b, h: (b, 0, h)),
            pl.BlockSpec((None, t, heads * B_DK), lambda b, h: (b, 0, nhb + h)),
            pl.BlockSpec((None, t, heads * B_DK), lambda b, h: (b, 0, h)),
            pl.BlockSpec((None, t, heads * B_DV), lambda b, h: (b, 0, h)),
            pl.BlockSpec((None, t, heads * B_DV), lambda b, h: (b, 0, h)),
            pl.BlockSpec((heads, 1, B_DV), lambda b, h: (h, 0, 0)),
            pl.BlockSpec((None, heads, B_DK, B_DV), lambda b, h: (b, h, 0, 0)),
        ],
        out_specs=(
            pl.BlockSpec((None, t, heads * B_DV), lambda b, h: (b, 0, h)),
            pl.BlockSpec((None, heads, B_DK, B_DV), lambda b, h: (b, h, 0, 0)),
        ),
        out_shape=(
            jax.ShapeDtypeStruct((bn, t, B_WIDTH), out_dtype),
            jax.ShapeDtypeStruct((bn, B_HEADS, B_DK, B_DV), F32),
        ),
        scratch_shapes=[pltpu.VMEM((heads, B_DV, B_DK), F32)],
        compiler_params=_params(("parallel", "parallel")),
        name="gla",
    )(qkb, qkb, lb, vb, rb, g, s0)


def _outproj_kernel(x_ref, ma_ref, mb_ref, wa_ref, wb_ref, g_ref, x1_ref, hm_ref):
    x1 = x_ref[...] + _dot(ma_ref[...].astype(BF16), wa_ref[...]) + _dot(mb_ref[...].astype(BF16), wb_ref[...])
    x1_ref[...] = x1
    hm_ref[...] = _rms_scale(x1, g_ref[...]).astype(BF16)


def _outproj(x, mix_a, mix_b, w_a, w_b, g, tm):
    t, d = x.shape
    row = lambda i: (i, 0)
    const = lambda i: (0, 0)
    return pl.pallas_call(
        _outproj_kernel,
        grid=(t // tm,),
        in_specs=[
            pl.BlockSpec((tm, d), row),
            pl.BlockSpec((tm, A_WIDTH), row),
            pl.BlockSpec((tm, B_WIDTH), row),
            pl.BlockSpec((A_WIDTH, d), const),
            pl.BlockSpec((B_WIDTH, d), const),
            pl.BlockSpec((1, d), const),
        ],
        out_specs=(pl.BlockSpec((tm, d), row), pl.BlockSpec((tm, d), row)),
        out_shape=(jax.ShapeDtypeStruct((t, d), F32), jax.ShapeDtypeStruct((t, d), BF16)),
        compiler_params=_params(("parallel",)),
        name="outproj",
    )(x, mix_a, mix_b, w_a, w_b, g)


def _mlp_kernel(x1_ref, hm_ref, wu_ref, wd_ref, y_ref):
    f = pl.program_id(1)

    @pl.when(f == 0)
    def _():
        y_ref[...] = x1_ref[...]

    u = jnp.square(jnp.maximum(_dot(hm_ref[...], wu_ref[...]), 0.0))
    y_ref[...] += _dot(u.astype(BF16), wd_ref[...])


def _mlp(x1, hm, w_up, w_down, tm, tf):
    t, d = x1.shape
    d_ff = w_up.shape[1]
    row = lambda i, f: (i, 0)
    return pl.pallas_call(
        _mlp_kernel,
        grid=(t // tm, d_ff // tf),
        in_specs=[
            pl.BlockSpec((tm, d), row),
            pl.BlockSpec((tm, d), row),
            pl.BlockSpec((d, tf), lambda i, f: (0, f)),
            pl.BlockSpec((tf, d), lambda i, f: (f, 0)),
        ],
        out_specs=pl.BlockSpec((tm, d), row),
        out_shape=jax.ShapeDtypeStruct((t, d), F32),
        compiler_params=_params(("parallel", "arbitrary")),
        name="mlp",
    )(x1, hm, w_up, w_down)


def _row_tile(t):
    return 512 if t % 512 == 0 else t


def _layer_weights(attn_norm_g, w_in, q_norm_g, k_norm_g, w_gate2, b_gate, gla_norm_g, w_out, mlp_norm_g,
                   w_up, w_down):
    w_g2 = jnp.pad(w_gate2, ((0, LANES - GATE_RANK), (0, 0)))
    return dict(
        attn_g=attn_norm_g[None, :],
        w_all=jnp.pad(w_in.astype(BF16), ((0, 0), (0, LANES - GATE_RANK))),
        w_g2=w_g2.astype(BF16),
        b_gate=b_gate[None, :],
        q_g=q_norm_g[None, :],
        k_g=k_norm_g[None, :],
        gla_g=gla_norm_g[:, None, :],
        w_out_a=w_out[:A_WIDTH].astype(BF16),
        w_out_b=w_out[A_WIDTH:].astype(BF16),
        mlp_g=mlp_norm_g[None, :],
        w_up=w_up.astype(BF16),
        w_down=w_down.astype(BF16),
    )


def _mixer_inputs(x2d, w, chunk):
    t = x2d.shape[0]
    tm = 256 if t % 256 == 0 else t
    return _inproj(x2d, w["attn_g"], w["w_all"], w["w_g2"], w["b_gate"], w["q_g"], w["k_g"], tm, chunk)


def _finish(x2d, mix_a, mix_b, w):
    t = x2d.shape[0]
    x1, hm = _outproj(x2d, mix_a, mix_b, w["w_out_a"], w["w_out_b"], w["mlp_g"], _row_tile(t))
    return _mlp(x1, hm, w["w_up"], w["w_down"], _row_tile(t), 1024)


def _prompt_layer(xp, w):
    bn, seq, d = xp.shape
    x2d = xp.reshape(bn * seq, d)
    assert seq % GLA_CHUNK == 0, "GLA chunks must not straddle batch rows"
    qa, ka, va, kwin, vwin, qkb, vb, rb, lb = _mixer_inputs(x2d, w, GLA_CHUNK)
    b3 = lambda a: a.reshape(bn, seq, a.shape[-1])
    oa = _attn_prompt(b3(qa), b3(ka), b3(va))
    s0 = jnp.zeros((bn, B_HEADS, B_DK, B_DV), F32)
    ob, s_fin = _gla(b3(qkb), b3(lb), b3(vb), b3(rb), w["gla_g"], s0, BF16)
    y = _finish(x2d, oa.reshape(bn * seq, A_WIDTH), ob.reshape(bn * seq, B_WIDTH), w)
    lp = min(A_BUF, seq)
    win_k = kwin.reshape(bn, seq, A_HEADS, HEAD_DIM)[:, seq - lp:]
    win_v = vwin.reshape(bn, seq, A_HEADS, HEAD_DIM)[:, seq - lp:]
    return y.reshape(bn, seq, d), win_k, win_v, s_fin


def _sample_layer(xs, cache_k, cache_v, state, w):
    bn, n_new, d = xs.shape
    x2d = xs.reshape(bn * n_new, d)
    assert n_new <= GLA_CHUNK, "the new tokens of one batch row form a single GLA chunk"
    qa, ka, va, kwin, vwin, qkb, vb, rb, lb = _mixer_inputs(x2d, w, n_new)

    def padded(a, rows, mode="constant"):
        a = a.reshape(bn, n_new, a.shape[-1])
        return jnp.pad(a, ((0, 0), (0, rows - n_new), (0, 0)), mode=mode)

    oa = _attn_sample(padded(qa, Q_ROWS), padded(ka, Q_ROWS), padded(va, Q_ROWS), cache_k, cache_v, n_new)
    c = GLA_CHUNK
    ob, s_new = _gla(padded(qkb, c), padded(lb, c, "edge"), padded(vb, c), padded(rb, c), w["gla_g"], state, F32)
    y = _finish(x2d, oa[:, :n_new].reshape(bn * n_new, A_WIDTH), ob[:, :n_new].reshape(bn * n_new, B_WIDTH), w)
    win_k = kwin.reshape(bn, n_new, A_HEADS, HEAD_DIM)
    win_v = vwin.reshape(bn, n_new, A_HEADS, HEAD_DIM)
    return y.reshape(bn, n_new, d), win_k, win_v, s_new


def kernel(x_prompt, x_sample, cache_win_k, cache_win_v, state_gla, attn_norm_g, w_in, q_norm_g, k_norm_g,
           w_gate2, b_gate, gla_norm_g, w_out, mlp_norm_g, w_up, w_down):
    depth = w_in.shape[0]
    xp, xs = x_prompt, x_sample
    pk, pv, ps, sk, sv, ss = [], [], [], [], [], []
    for l in range(depth):
        w = _layer_weights(attn_norm_g[l], w_in[l], q_norm_g[l], k_norm_g[l], w_gate2[l], b_gate[l],
                           gla_norm_g[l], w_out[l], mlp_norm_g[l], w_up[l], w_down[l])
        xp, k_l, v_l, s_l = _prompt_layer(xp, w)
        pk.append(k_l)
        pv.append(v_l)
        ps.append(s_l)
        xs, k_l, v_l, s_l = _sample_layer(xs, cache_win_k[l], cache_win_v[l], state_gla[l], w)
        sk.append(k_l)
        sv.append(v_l)
        ss.append(s_l)
    return (xp, xs, jnp.stack(pk), jnp.stack(pv), jnp.stack(ps), jnp.stack(sk), jnp.stack(sv), jnp.stack(ss))
```
